```python
import functools
import jax, jax.numpy as jnp
from jax import lax
import numpy as np

D_MODEL = 2048
BATCH = 2
SEQ = 4096
DEPTH = 2
DEC_BATCH = 8
DEC_SEQ = 1
PAST_LEN = 16384
PAGE_SIZE = 128

N_HEADS = 16
HEAD_DIM = D_MODEL // N_HEADS
CONV_W = 3
D_FF = ((8 * D_MODEL // 3 + 255) // 256) * 256
N_EXPERTS = 8
TOP_K = 2
EXPERT_FF = D_FF // 4
Q_BLOCK = 128
N_A_LAYERS = DEPTH // 2
N_B_LAYERS = DEPTH - N_A_LAYERS
N_DENSE = (DEPTH + 1) // 2
N_MOE = DEPTH // 2
EPS = 1e-6
ATTN_SCALE = HEAD_DIM ** -0.5

kernel_name = 'yoco_conv_fox_moe_step'


def _rms(x, g):
    xf = x.astype(jnp.float32)
    y = xf * lax.rsqrt(jnp.mean(xf * xf, axis=-1, keepdims=True) + EPS)
    return (y * g.astype(jnp.float32)).astype(x.dtype)


def _conv_mixer(h, prev, w_in, conv_w, w_out):
    T = h.shape[1]
    b_g, c_g, u = jnp.split(h @ w_in, 3, axis=-1)
    z = c_g * u
    zp = jnp.concatenate([prev.astype(z.dtype), z], axis=1)
    y = zp[:, 0:T] * conv_w[0]
    for i in range(1, CONV_W):
        y = y + zp[:, i:i + T] * conv_w[i]
    return (b_g * y) @ w_out, zp[:, -(CONV_W - 1):]


def _swiglu(h, wg, wu, wd):
    return (jax.nn.silu(h @ wg) * (h @ wu)) @ wd


def _moe(h, w_router, b_router, wg, wu, wd):
    logits = (h @ w_router).astype(jnp.float32) + b_router.astype(jnp.float32)
    probs = jax.nn.softmax(logits, axis=-1)
    topv, topi = lax.top_k(probs, TOP_K)
    topv = topv / jnp.sum(topv, axis=-1, keepdims=True)
    gates = jnp.sum(jax.nn.one_hot(topi, N_EXPERTS, dtype=jnp.float32) * topv[..., None], axis=-2)
    a = jnp.einsum('btd,edf->btef', h, wg)
    u = jnp.einsum('btd,edf->btef', h, wu)
    hid = jax.nn.silu(a) * u * gates[..., None].astype(h.dtype)
    return jnp.einsum('btef,efd->btd', hid, wd)


def _shared_kv(x, c, g_kv, w_mod_kv, b_mod_kv, w_kv, b_f):
    B, T, _ = x.shape
    shift, scale = jnp.split((c @ w_mod_kv + b_mod_kv)[:, None, :], 2, axis=-1)
    h = _rms(x, g_kv) * (1 + scale) + shift
    k, v, f = jnp.split(h @ w_kv, [D_MODEL, 2 * D_MODEL], axis=-1)
    logf = jax.nn.log_sigmoid((f + b_f).astype(jnp.float32))
    return (k.reshape(B, T, N_HEADS, HEAD_DIM), v.reshape(B, T, N_HEADS, HEAD_DIM), logf)


def _attend_prompt(q, k, v, logf):
    B, T = q.shape[0], q.shape[1]
    F = jnp.cumsum(logf.astype(jnp.float32), axis=1)
    Fk = jnp.transpose(F, (0, 2, 1))[:, :, None, :]
    kpos = jnp.arange(T)

    def blk(i):
        start = i * Q_BLOCK
        qs = lax.dynamic_slice_in_dim(q, start, Q_BLOCK, axis=1)
        fs = lax.dynamic_slice_in_dim(F, start, Q_BLOCK, axis=1)
        qpos = start + jnp.arange(Q_BLOCK)
        s = jnp.einsum('bqhd,bkhd->bhqk', qs, k, preferred_element_type=jnp.float32) * ATTN_SCALE
        s = s + jnp.transpose(fs, (0, 2, 1))[..., None] - Fk
        s = jnp.where(qpos[:, None] >= kpos[None, :], s, -jnp.inf)
        p = jax.nn.softmax(s, axis=-1).astype(v.dtype)
        return jnp.einsum('bhqk,bkhd->bqhd', p, v)

    o = lax.map(blk, jnp.arange(T // Q_BLOCK))
    return jnp.moveaxis(o, 0, 1).reshape(B, T, D_MODEL)


def _attend_sample(q, k, v, logf, kp, vp, logf_past):
    B, T = q.shape[0], q.shape[1]
    P = kp.shape[1]
    Fp = jnp.cumsum(logf_past.astype(jnp.float32), axis=1)
    Fn = Fp[:, -1:] + jnp.cumsum(logf.astype(jnp.float32), axis=1)
    fq = jnp.transpose(Fn, (0, 2, 1))[..., None]
    sp = jnp.einsum('bqhd,bkhd->bhqk', q, kp, preferred_element_type=jnp.float32) * ATTN_SCALE
    sp = sp + fq - jnp.transpose(Fp, (0, 2, 1))[:, :, None, :]
    sn = jnp.einsum('bqhd,bkhd->bhqk', q, k, preferred_element_type=jnp.float32) * ATTN_SCALE
    sn = sn + fq - jnp.transpose(Fn, (0, 2, 1))[:, :, None, :]
    tpos = jnp.arange(T)
    sn = jnp.where(tpos[:, None] >= tpos[None, :], sn, -jnp.inf)
    p = jax.nn.softmax(jnp.concatenate([sp, sn], axis=-1), axis=-1).astype(v.dtype)
    o = jnp.einsum('bhqk,bkhd->bqhd', p[..., :P], vp) + jnp.einsum('bhqk,bkhd->bqhd', p[..., P:], v)
    return o.reshape(B, T, D_MODEL)


def _trunk(x, c, conv_prev, attend, p):
    B, T, _ = x.shape
    new_conv = []
    k = v = logf = None
    for l in range(DEPTH):
        mod = (c @ p['w_mod'][l] + p['b_mod'][l])[:, None, :]
        sh_m, sc_m, g_m, sh_f, sc_f, g_f = jnp.split(mod, 6, axis=-1)
        g = p['g_norm'][l]
        h = _rms(x, g[0]) * (1 + sc_m) + sh_m
        if l < N_A_LAYERS:
            o, st = _conv_mixer(h, conv_prev[l], p['w_in_a'][l], p['conv_w'][l], p['w_out_a'][l])
            new_conv.append(st)
        else:
            if l == N_A_LAYERS:
                k, v, logf = _shared_kv(x, c, p['g_kv'], p['w_mod_kv'], p['b_mod_kv'], p['w_kv'], p['b_f'])
            b = l - N_A_LAYERS
            q = (h @ p['w_q'][b]).reshape(B, T, N_HEADS, HEAD_DIM)
            o = attend(q, k, v, logf) @ p['w_o'][b]
        x = x + g_m * _rms(o, g[1])
        h = _rms(x, g[2]) * (1 + sc_f) + sh_f
        if l % 2 == 0:
            i = l // 2
            f = _swiglu(h, p['w_gate_d'][i], p['w_up_d'][i], p['w_down_d'][i])
        else:
            i = l // 2
            f = _moe(h, p['w_router'][i], p['b_router'][i], p['w_gate_e'][i], p['w_up_e'][i], p['w_down_e'][i])
        x = x + g_f * _rms(f, g[3])
    return x, jnp.stack(new_conv), k, v, logf


def setup_inputs(seed: int = 0) -> dict:
    key = jax.random.key(seed)
    ks = jax.random.split(key, 40)
    D = D_MODEL
    n_pages = PAST_LEN // PAGE_SIZE
    n_used = DEC_BATCH * n_pages
    n_phys = n_used + max(1, n_used // 4)
    nrm = lambda k, s, sc: jax.random.normal(k, s, jnp.float32) * sc
    b_f = 2.0 + 5.0 * jax.random.uniform(ks[0], (N_HEADS,), jnp.float32)
    page_table = jax.random.permutation(ks[1], n_phys)[:n_used].reshape(DEC_BATCH, n_pages).astype(jnp.int32)
    return {
        'x_prompt': nrm(ks[2], (BATCH, SEQ, D), 1.0),
        'x_sample': nrm(ks[3], (DEC_BATCH, DEC_SEQ, D), 1.0),
        'state_conv': nrm(ks[4], (N_A_LAYERS, DEC_BATCH, CONV_W - 1, D), 1.0),
        'cache_k': nrm(ks[5], (n_phys, PAGE_SIZE, N_HEADS, HEAD_DIM), 1.0),
        'cache_v': nrm(ks[6], (n_phys, PAGE_SIZE, N_HEADS, HEAD_DIM), 1.0),
        'cache_logf': jax.nn.log_sigmoid(b_f + nrm(ks[7], (n_phys, PAGE_SIZE, N_HEADS), 0.5)),
        'page_table': page_table,
        'c_prompt': nrm(ks[8], (BATCH, D), 1.0),
        'c_sample': nrm(ks[9], (DEC_BATCH, D), 1.0),
        'w_mod': nrm(ks[10], (DEPTH, D, 6 * D), 0.5 * D ** -0.5),
        'b_mod': nrm(ks[11], (DEPTH, 6 * D), 0.02),
        'g_norm': 1.0 + nrm(ks[12], (DEPTH, 4, D), 0.02),
        'w_in_a': nrm(ks[13], (N_A_LAYERS, D, 3 * D), D ** -0.5),
        'conv_w': nrm(ks[14], (N_A_LAYERS, CONV_W, D), CONV_W ** -0.5),
        'w_out_a': nrm(ks[15], (N_A_LAYERS, D, D), D ** -0.5),
        'g_kv': 1.0 + nrm(ks[16], (D,), 0.02),
        'w_mod_kv': nrm(ks[17], (D, 2 * D), 0.5 * D ** -0.5),
        'b_mod_kv': nrm(ks[18], (2 * D,), 0.02),
        'w_kv': nrm(ks[19], (D, 2 * D + N_HEADS), D ** -0.5),
        'b_f': b_f,
        'w_q': nrm(ks[20], (N_B_LAYERS, D, D), D ** -0.5),
        'w_o': nrm(ks[21], (N_B_LAYERS, D, D), D ** -0.5),
        'w_gate_d': nrm(ks[22], (N_DENSE, D, D_FF), D ** -0.5),
        'w_up_d': nrm(ks[23], (N_DENSE, D, D_FF), D ** -0.5),
        'w_down_d': nrm(ks[24], (N_DENSE, D_FF, D), D_FF ** -0.5),
        'w_router': nrm(ks[25], (N_MOE, D, N_EXPERTS), D ** -0.5),
        'b_router': nrm(ks[26], (N_MOE, N_EXPERTS), 0.01),
        'w_gate_e': nrm(ks[27], (N_MOE, N_EXPERTS, D, EXPERT_FF), D ** -0.5),
        'w_up_e': nrm(ks[28], (N_MOE, N_EXPERTS, D, EXPERT_FF), D ** -0.5),
        'w_down_e': nrm(ks[29], (N_MOE, N_EXPERTS, EXPERT_FF, D), EXPERT_FF ** -0.5),
    }


def reference(x_prompt, x_sample, state_conv, cache_k, cache_v, cache_logf, page_table, c_prompt, c_sample,
              w_mod, b_mod, g_norm, w_in_a, conv_w, w_out_a, g_kv, w_mod_kv, b_mod_kv, w_kv, b_f,
              w_q, w_o, w_gate_d, w_up_d, w_down_d, w_router, b_router, w_gate_e, w_up_e, w_down_e):
    p = {'w_mod': w_mod, 'b_mod': b_mod, 'g_norm': g_norm, 'w_in_a': w_in_a, 'conv_w': conv_w,
         'w_out_a': w_out_a, 'g_kv': g_kv, 'w_mod_kv': w_mod_kv, 'b_mod_kv': b_mod_kv, 'w_kv': w_kv,
         'b_f': b_f, 'w_q': w_q, 'w_o': w_o, 'w_gate_d': w_gate_d, 'w_up_d': w_up_d,
         'w_down_d': w_down_d, 'w_router': w_router, 'b_router': b_router, 'w_gate_e': w_gate_e,
         'w_up_e': w_up_e, 'w_down_e': w_down_e}
    conv0 = jnp.zeros((N_A_LAYERS, x_prompt.shape[0], CONV_W - 1, D_MODEL), x_prompt.dtype)
    y_p, conv_p, k_p, v_p, lf_p = _trunk(x_prompt, c_prompt, conv0, _attend_prompt, p)
    nb = x_sample.shape[0]
    kp = cache_k[page_table].reshape(nb, -1, N_HEADS, HEAD_DIM)
    vp = cache_v[page_table].reshape(nb, -1, N_HEADS, HEAD_DIM)
    lfp = cache_logf[page_table].reshape(nb, -1, N_HEADS)
    attend_s = functools.partial(_attend_sample, kp=kp, vp=vp, logf_past=lfp)
    y_s, conv_s, k_s, v_s, lf_s = _trunk(x_sample, c_sample, state_conv, attend_s, p)
    return (y_p, y_s, conv_p, conv_s, k_p, v_p, lf_p, k_s, v_s, lf_s)
```

```python
import functools

import jax
import jax.numpy as jnp
from jax import lax
from jax.experimental import pallas as pl
from jax.experimental.pallas import tpu as pltpu

D = 2048
N_HEADS = 16
HEAD_DIM = 128
D_FF = 5632
N_EXPERTS = 8
EXPERT_FF = 1408
PAGE = 128
EPS = 1e-6
ATTN_SCALE = HEAD_DIM ** -0.5
LANES = 128
SAMPLE_ROWS = 16
VMEM_LIMIT = 56 * 1024 * 1024

BF = jnp.bfloat16
F32 = jnp.float32
NT_DIMS = (((1,), (1,)), ((), ()))


def _cp(*sem):
    return pltpu.CompilerParams(dimension_semantics=sem, vmem_limit_bytes=VMEM_LIMIT)


def _dot(a, b):
    return jnp.dot(a, b, preferred_element_type=F32)


class Geo:
    def __init__(self, m, rows_per_batch, mod_rows):
        self.m = m
        self.tb = rows_per_batch
        self.r = mod_rows

    def tile(self, pref):
        return min(pref, self.m)

    def mod_spec(self, tm, k):
        tpb = self.tb // tm
        return pl.BlockSpec((1, self.r, D), lambda i, *_: (i // tpb, 0, k))


def _vec_spec():
    return pl.BlockSpec((1, D), lambda i, *_: (0, 0))


def _modvec_kernel(c_ref, w_ref, b_ref, o_ref):
    o_ref[...] = _dot(c_ref[...].astype(BF), w_ref[...].astype(BF)) + b_ref[...]


def _modvec(c16, w, b):
    nl, _, n = w.shape
    tn = 1024
    return pl.pallas_call(
        _modvec_kernel,
        grid=(nl, n // tn),
        in_specs=[pl.BlockSpec((SAMPLE_ROWS, D), lambda l, j: (0, 0)),
                  pl.BlockSpec((None, D, tn), lambda l, j: (l, 0, j)),
                  pl.BlockSpec((None, 1, tn), lambda l, j: (l, 0, j))],
        out_specs=pl.BlockSpec((None, SAMPLE_ROWS, tn), lambda l, j: (l, 0, j)),
        out_shape=jax.ShapeDtypeStruct((nl, SAMPLE_ROWS, n), F32),
        compiler_params=_cp("arbitrary", "arbitrary"),
        name="modvec",
    )(c16, w, b.reshape(nl, 1, n))


def _norm_mod_kernel(x_ref, g_ref, sc_ref, sh_ref, h_ref):
    x = x_ref[...]
    r = lax.rsqrt(jnp.mean(x * x, axis=-1, keepdims=True) + EPS)
    h = (x * r * g_ref[...]) * (1.0 + sc_ref[0]) + sh_ref[0]
    h_ref[...] = h.astype(BF)


def _norm_mod(x, g, mod, sc_k, sh_k, geo):
    tm = geo.tile(512)
    row = pl.BlockSpec((tm, D), lambda i: (i, 0))
    return pl.pallas_call(
        _norm_mod_kernel,
        grid=(geo.m // tm,),
        in_specs=[row, _vec_spec(), geo.mod_spec(tm, sc_k), geo.mod_spec(tm, sh_k)],
        out_specs=row,
        out_shape=jax.ShapeDtypeStruct((geo.m, D), BF),
        compiler_params=_cp("arbitrary"),
        name="norm_mod",
    )(x, g.reshape(1, D), mod, mod)


def _top2_gates(h, wr, br):
    logits = jnp.dot(h, wr, preferred_element_type=F32, precision=lax.Precision.HIGHEST) + br
    lane = lax.broadcasted_iota(jnp.int32, logits.shape, 1)
    valid = lane < N_EXPERTS
    lg = jnp.where(valid, logits, -jnp.inf)
    e = jnp.exp(lg - jnp.max(lg, axis=-1, keepdims=True))
    p = e / jnp.sum(e, axis=-1, keepdims=True)
    p = jnp.where(valid, p, -1.0)
    v1 = jnp.max(p, axis=-1, keepdims=True)
    i1 = jnp.min(jnp.where(p == v1, lane, LANES), axis=-1, keepdims=True)
    p2 = jnp.where(lane == i1, -1.0, p)
    v2 = jnp.max(p2, axis=-1, keepdims=True)
    i2 = jnp.min(jnp.where(p2 == v2, lane, LANES), axis=-1, keepdims=True)
    den = v1 + v2
    return jnp.where(lane == i1, v1 / den, 0.0) + jnp.where(lane == i2, v2 / den, 0.0)


def _epi_kernel(*refs, n_pro, router):
    x_ref, f_ref, gpost_ref, gate_ref = refs[:4]
    pos = 4
    pro = [refs[pos + 3 * k: pos + 3 * k + 3] for k in range(n_pro)]
    pos += 3 * n_pro
    if router:
        wr_ref, br_ref = refs[pos:pos + 2]
        pos += 2
    xo_ref = refs[pos]
    h_refs = refs[pos + 1: pos + 1 + n_pro]
    f = f_ref[...]
    rf = lax.rsqrt(jnp.mean(f * f, axis=-1, keepdims=True) + EPS)
    xn = x_ref[...] + gate_ref[0] * (f * rf * gpost_ref[...])
    xo_ref[...] = xn
    if n_pro:
        xr = xn * lax.rsqrt(jnp.mean(xn * xn, axis=-1, keepdims=True) + EPS)
        for k, (g_ref, sc_ref, sh_ref) in enumerate(pro):
            h = (xr * g_ref[...]) * (1.0 + sc_ref[0]) + sh_ref[0]
            h_refs[k][...] = h.astype(BF)
            if router and k == 0:
                refs[pos + 1 + n_pro][...] = _top2_gates(h, wr_ref[...], br_ref[...])


def _epi(x, f, g_post, gate, pro, geo, router=None):
    mod_gate, gate_k = gate
    tm = geo.tile(256)
    row = pl.BlockSpec((tm, D), lambda i: (i, 0))
    in_specs = [row, row, _vec_spec(), geo.mod_spec(tm, gate_k)]
    args = [x, f, g_post.reshape(1, D), mod_gate]
    for g, mod, sc_k, sh_k in pro:
        in_specs += [_vec_spec(), geo.mod_spec(tm, sc_k), geo.mod_spec(tm, sh_k)]
        args += [g.reshape(1, D), mod, mod]
    out_specs = [row] + [row] * len(pro)
    out_shape = [jax.ShapeDtypeStruct((geo.m, D), F32)] + [jax.ShapeDtypeStruct((geo.m, D), BF)] * len(pro)
    if router is not None:
        wr, br = router
        in_specs += [pl.BlockSpec((D, LANES), lambda i: (0, 0)), pl.BlockSpec((1, LANES), lambda i: (0, 0))]
        args += [wr, br]
        out_specs.append(pl.BlockSpec((tm, LANES), lambda i: (i, 0)))
        out_shape.append(jax.ShapeDtypeStruct((geo.m, LANES), F32))
    return pl.pallas_call(
        functools.partial(_epi_kernel, n_pro=len(pro), router=router is not None),
        grid=(geo.m // tm,),
        in_specs=in_specs,
        out_specs=out_specs,
        out_shape=out_shape,
        compiler_params=_cp("arbitrary"),
        name="epi",
    )(*args)


def _mm_kernel(a_ref, w_ref, o_ref):
    o_ref[...] = _dot(a_ref[...], w_ref[...]).astype(o_ref.dtype)


def _mm(a, w, out_dtype, geo):
    n = w.shape[1]
    tm, tn = geo.tile(1024), min(1024, n)
    return pl.pallas_call(
        _mm_kernel,
        grid=(geo.m // tm, n // tn),
        in_specs=[pl.BlockSpec((tm, D), lambda i, j: (i, 0)), pl.BlockSpec((D, tn), lambda i, j: (0, j))],
        out_specs=pl.BlockSpec((tm, tn), lambda i, j: (i, j)),
        out_shape=jax.ShapeDtypeStruct((geo.m, n), out_dtype),
        compiler_params=_cp("arbitrary", "arbitrary"),
        name="mm",
    )(a, w)


def _logf_kernel(a_ref, w_ref, b_ref, o_ref):
    f = _dot(a_ref[...], w_ref[...]) + b_ref[...]
    o_ref[...] = jnp.minimum(f, 0.0) - jnp.log1p(jnp.exp(-jnp.abs(f)))


def _logf(a, w, b, geo):
    tm = geo.tile(1024)
    return pl.pallas_call(
        _logf_kernel,
        grid=(geo.m // tm,),
        in_specs=[pl.BlockSpec((tm, D), lambda i: (i, 0)), pl.BlockSpec((D, LANES), lambda i: (0, 0)),
                  pl.BlockSpec((1, LANES), lambda i: (0, 0))],
        out_specs=pl.BlockSpec((tm, LANES), lambda i: (i, 0)),
        out_shape=jax.ShapeDtypeStruct((geo.m, LANES), F32),
        compiler_params=_cp("arbitrary"),
        name="logf",
    )(a, w, b)


def _conv_in_kernel(a_ref, wb_ref, wc_ref, wu_ref, cw_ref, prev_ref, y_ref, st_ref, carry_ref, *, tpb):
    i, j = pl.program_id(0), pl.program_id(1)
    a = a_ref[...]
    bg = _dot(a, wb_ref[...])
    z = _dot(a, wc_ref[...]) * _dot(a, wu_ref[...])
    tm = z.shape[0]

    @pl.when(i % tpb == 0)
    def _():
        carry_ref[j] = prev_ref[0]

    prev = carry_ref[j]
    p0, p1 = prev[0:1], prev[1:2]
    row = lax.broadcasted_iota(jnp.int32, z.shape, 0)
    z1 = jnp.where(row == 0, p1, pltpu.roll(z, 1, 0))
    z2 = jnp.where(row == 0, p0, jnp.where(row == 1, p1, pltpu.roll(z, 2, 0)))
    cw = cw_ref[...]
    y = z2 * cw[0:1] + z1 * cw[1:2] + z * cw[2:3]
    y_ref[...] = (bg * y).astype(BF)
    last = z[tm - 2:tm]
    carry_ref[j] = last
    st_ref[0] = last


def _conv_in(h, w_in, conv_w, prev, geo):
    tm, tn = geo.tile(1024), 512
    nj = D // tn
    tpb = geo.tb // tm
    wspec = lambda off: pl.BlockSpec((D, tn), lambda i, j: (0, off * nj + j))
    y, tails = pl.pallas_call(
        functools.partial(_conv_in_kernel, tpb=tpb),
        grid=(geo.m // tm, nj),
        in_specs=[pl.BlockSpec((tm, D), lambda i, j: (i, 0)), wspec(0), wspec(1), wspec(2),
                  pl.BlockSpec((3, tn), lambda i, j: (0, j)),
                  pl.BlockSpec((1, 2, tn), lambda i, j: (i // tpb, 0, j))],
        out_specs=[pl.BlockSpec((tm, tn), lambda i, j: (i, j)),
                   pl.BlockSpec((1, 2, tn), lambda i, j: (i, 0, j))],
        out_shape=[jax.ShapeDtypeStruct((geo.m, D), BF),
                   jax.ShapeDtypeStruct((geo.m // tm, 2, D), F32)],
        scratch_shapes=[pltpu.VMEM((nj, 2, tn), F32)],
        compiler_params=_cp("arbitrary", "arbitrary"),
        name="conv_in",
    )(h, w_in, w_in, w_in, conv_w, prev)
    return y, tails[tpb - 1::tpb]


def _conv_in_step_kernel(a_ref, wb_ref, wc_ref, wu_ref, cw_ref, p0_ref, p1_ref, y_ref, z_ref):
    a = a_ref[...]
    bg = _dot(a, wb_ref[...])
    z = _dot(a, wc_ref[...]) * _dot(a, wu_ref[...])
    cw = cw_ref[...]
    y = p0_ref[...] * cw[0:1] + p1_ref[...] * cw[1:2] + z * cw[2:3]
    y_ref[...] = (bg * y).astype(BF)
    z_ref[...] = z


def _conv_in_step(h, w_in, conv_w, p0, p1, geo):
    tm, tn = geo.m, 512
    nj = D // tn
    wspec = lambda off: pl.BlockSpec((D, tn), lambda j: (0, off * nj + j))
    blk = pl.BlockSpec((tm, tn), lambda j: (0, j))
    return pl.pallas_call(
        _conv_in_step_kernel,
        grid=(nj,),
        in_specs=[pl.BlockSpec((tm, D), lambda j: (0, 0)), wspec(0), wspec(1), wspec(2),
                  pl.BlockSpec((3, tn), lambda j: (0, j)), blk, blk],
        out_specs=[blk, blk],
        out_shape=[jax.ShapeDtypeStruct((tm, D), BF), jax.ShapeDtypeStruct((tm, D), F32)],
        compiler_params=_cp("arbitrary"),
        name="conv_in_step",
    )(h, w_in, w_in, w_in, conv_w, p0, p1)


def _mlp_kernel(a_ref, wg_ref, wu_ref, wd_ref, *rest, gated):
    o_ref = rest[-1]
    f = pl.program_id(1)
    a = a_ref[...]
    g = _dot(a, wg_ref[...])
    u = _dot(a, wu_ref[...])
    hid = (g * (1.0 / (1.0 + jnp.exp(-g)))) * u
    if gated:
        gates = rest[0][...]
        lane = lax.broadcasted_iota(jnp.int32, gates.shape, 1)
        hid = hid * jnp.sum(jnp.where(lane == f, gates, 0.0), axis=-1, keepdims=True)
    part = _dot(hid.astype(BF), wd_ref[...])

    @pl.when(f == 0)
    def _():
        o_ref[...] = part

    @pl.when(f > 0)
    def _():
        o_ref[...] += part


def _mlp(h, wg, wu, wd, geo):
    tm, tf = geo.tile(1024), 512
    return pl.pallas_call(
        functools.partial(_mlp_kernel, gated=False),
        grid=(geo.m // tm, D_FF // tf),
        in_specs=[pl.BlockSpec((tm, D), lambda i, f: (i, 0)),
                  pl.BlockSpec((D, tf), lambda i, f: (0, f)),
                  pl.BlockSpec((D, tf), lambda i, f: (0, f)),
                  pl.BlockSpec((tf, D), lambda i, f: (f, 0))],
        out_specs=pl.BlockSpec((tm, D), lambda i, f: (i, 0)),
        out_shape=jax.ShapeDtypeStruct((geo.m, D), F32),
        compiler_params=_cp("arbitrary", "arbitrary"),
        name="mlp",
    )(h, wg, wu, wd)


def _moe(h, gates, wg, wu, wd, geo):
    tm = geo.tile(256)
    return pl.pallas_call(
        functools.partial(_mlp_kernel, gated=True),
        grid=(geo.m // tm, N_EXPERTS),
        in_specs=[pl.BlockSpec((tm, D), lambda i, e: (i, 0)),
                  pl.BlockSpec((None, D, EXPERT_FF), lambda i, e: (e, 0, 0)),
                  pl.BlockSpec((None, D, EXPERT_FF), lambda i, e: (e, 0, 0)),
                  pl.BlockSpec((None, EXPERT_FF, D), lambda i, e: (e, 0, 0)),
                  pl.BlockSpec((tm, LANES), lambda i, e: (i, 0))],
        out_specs=pl.BlockSpec((tm, D), lambda i, e: (i, 0)),
        out_shape=jax.ShapeDtypeStruct((geo.m, D), F32),
        compiler_params=_cp("arbitrary", "arbitrary"),
        name="moe",
    )(h, wg, wu, wd, gates)


def _cumsum_lanes(x):
    n = x.shape[-1]
    lane = lax.broadcasted_iota(jnp.int32, x.shape, x.ndim - 1)
    s = 1
    while s < n:
        x = x + jnp.where(lane >= s, pltpu.roll(x, s, x.ndim - 1), 0.0)
        s *= 2
    return x


def _cumsum_kernel(x_ref, o_ref):
    o_ref[0] = _cumsum_lanes(x_ref[0])


def _cumsum(lf_t):
    b, h, t = lf_t.shape
    spec = pl.BlockSpec((1, h, t), lambda i: (i, 0, 0))
    return pl.pallas_call(
        _cumsum_kernel, grid=(b,), in_specs=[spec], out_specs=spec,
        out_shape=jax.ShapeDtypeStruct(lf_t.shape, F32),
        compiler_params=_cp("arbitrary"), name="cumsum",
    )(lf_t)


def _flash_kernel(q_ref, k_ref, v_ref, fk_ref, o_ref, kb_ref, vb_ref, *, tq):
    qi = pl.program_id(2)

    @pl.when(qi == 0)
    def _():
        kb_ref[...] = k_ref[...].astype(BF)
        vb_ref[...] = v_ref[...].astype(BF)

    q = q_ref[...]

    def step(kj, carry, masked):
        m, l, acc = carry
        start = pl.multiple_of(kj * tq, tq)
        k = kb_ref[pl.ds(start, tq), :]
        v = vb_ref[pl.ds(start, tq), :]
        s = lax.dot_general(q, k, NT_DIMS, preferred_element_type=F32) * ATTN_SCALE - fk_ref[0, kj]
        if masked:
            row = lax.broadcasted_iota(jnp.int32, s.shape, 0)
            col = lax.broadcasted_iota(jnp.int32, s.shape, 1)
            s = jnp.where(col <= row, s, -jnp.inf)
        m_new = jnp.maximum(m, jnp.max(s, axis=-1, keepdims=True))
        alpha = jnp.exp(m - m_new)
        p = jnp.exp(s - m_new)
        l = alpha * l + jnp.sum(p, axis=-1, keepdims=True)
        acc = alpha * acc + _dot(p.astype(BF), v)
        return m_new, l, acc

    init = (jnp.full((tq, 1), -jnp.inf, F32), jnp.zeros((tq, 1), F32), jnp.zeros((tq, HEAD_DIM), F32))
    carry = lax.fori_loop(0, qi, lambda kj, c: step(kj, c, False), init)
    _, l, acc = step(qi, carry, True)
    o_ref[...] = (acc / l).astype(o_ref.dtype)


def _flash(q, k, v, fk, nb, t):
    tq = 512
    nq = t // tq
    return pl.pallas_call(
        functools.partial(_flash_kernel, tq=tq),
        grid=(nb, N_HEADS, nq),
        in_specs=[pl.BlockSpec((tq, HEAD_DIM), lambda b, h, i: (b * nq + i, h)),
                  pl.BlockSpec((t, HEAD_DIM), lambda b, h, i: (b, h)),
                  pl.BlockSpec((t, HEAD_DIM), lambda b, h, i: (b, h)),
                  pl.BlockSpec((1, nq, 1, tq), lambda b, h, i: (b * N_HEADS + h, 0, 0, 0))],
        out_specs=pl.BlockSpec((tq, HEAD_DIM), lambda b, h, i: (b * nq + i, h)),
        out_shape=jax.ShapeDtypeStruct((nb * t, D), BF),
        scratch_shapes=[pltpu.VMEM((t, HEAD_DIM), BF), pltpu.VMEM((t, HEAD_DIM), BF)],
        compiler_params=_cp("arbitrary", "arbitrary", "arbitrary"),
        name="flash",
    )(q, k, v, fk)


def _decode_kernel(pt_ref, q_ref, kn_ref, vn_ref, lfn_ref, kp_ref, vp_ref, lfp_ref, o_ref,
                   qb_ref, m_ref, l_ref, acc_ref, fc_ref):
    p = pl.program_id(1)
    head = lax.broadcasted_iota(jnp.int32, (N_HEADS, D), 0)
    diag = head == lax.broadcasted_iota(jnp.int32, (N_HEADS, D), 1) // HEAD_DIM

    @pl.when(p == 0)
    def _():
        qb_ref[...] = jnp.where(diag, q_ref[0], 0.0).astype(BF)
        m_ref[...] = jnp.full(m_ref.shape, -jnp.inf, F32)
        l_ref[...] = jnp.zeros(l_ref.shape, F32)
        acc_ref[...] = jnp.zeros(acc_ref.shape, F32)
        fc_ref[...] = jnp.zeros(fc_ref.shape, F32)

    s = lax.dot_general(qb_ref[...], kp_ref[0].astype(BF), NT_DIMS, preferred_element_type=F32)
    fp = _cumsum_lanes(lfp_ref[0]) + fc_ref[...]
    s = s * ATTN_SCALE - fp
    m_old = m_ref[...]
    m_new = jnp.maximum(m_old, jnp.max(s, axis=-1, keepdims=True))
    alpha = jnp.exp(m_old - m_new)
    pr = jnp.exp(s - m_new)
    l_new = alpha * l_ref[...] + jnp.sum(pr, axis=-1, keepdims=True)
    acc_new = alpha[:, 0:1] * acc_ref[...] + _dot(pr.astype(BF), vp_ref[0].astype(BF))
    fc_new = jnp.broadcast_to(fp[:, PAGE - 1:PAGE], fp.shape)
    m_ref[...] = m_new
    l_ref[...] = l_new
    acc_ref[...] = acc_new
    fc_ref[...] = fc_new

    @pl.when(p == pl.num_programs(1) - 1)
    def _():
        kn = kn_ref[0].astype(BF).astype(F32)
        vn = vn_ref[0].astype(BF).astype(F32)
        s_n = jnp.sum(qb_ref[...].astype(F32) * kn, axis=-1, keepdims=True) * ATTN_SCALE
        s_n = s_n - (fc_new[:, 0:1] + lfn_ref[0])
        m1 = m_new[:, 0:1]
        m2 = jnp.maximum(m1, s_n)
        a2 = jnp.exp(m1 - m2)
        pn = jnp.exp(s_n - m2)
        l2 = a2 * l_new[:, 0:1] + pn
        acc2 = a2 * acc_new + pn.astype(BF).astype(F32) * vn
        o = jnp.sum(jnp.where(diag, acc2 / l2, 0.0), axis=0, keepdims=True)
        o_ref[0] = o.astype(o_ref.dtype)


def _decode(page_table, q, kn, vn, lfn, cache_k, cache_v, cache_lf_t):
    nb, n_pages = page_table.shape
    tok = pl.BlockSpec((1, 1, D), lambda b, p, pt: (b, 0, 0))
    page = pl.BlockSpec((1, PAGE, D), lambda b, p, pt: (pt[b, p], 0, 0))
    grid_spec = pltpu.PrefetchScalarGridSpec(
        num_scalar_prefetch=1,
        grid=(nb, n_pages),
        in_specs=[tok, tok, tok,
                  pl.BlockSpec((1, N_HEADS, 1), lambda b, p, pt: (b, 0, 0)),
                  page, page,
                  pl.BlockSpec((1, N_HEADS, PAGE), lambda b, p, pt: (pt[b, p], 0, 0))],
        out_specs=tok,
        scratch_shapes=[pltpu.VMEM((N_HEADS, D), BF),
                        pltpu.VMEM((N_HEADS, PAGE), F32),
                        pltpu.VMEM((N_HEADS, PAGE), F32),
                        pltpu.VMEM((N_HEADS, D), F32),
                        pltpu.VMEM((N_HEADS, PAGE), F32)],
    )
    return pl.pallas_call(
        _decode_kernel,
        grid_spec=grid_spec,
        out_shape=jax.ShapeDtypeStruct((nb, 1, D), BF),
        compiler_params=_cp("arbitrary", "arbitrary"),
        name="decode",
    )(page_table, q, kn, vn, lfn, cache_k, cache_v, cache_lf_t)


def _trunk(x, geo, mod0, mod1, modkv, w, conv_fn, attend_fn):
    g_norm = w["g_norm"]
    h = _norm_mod(x, g_norm[0, 0], mod0, 1, 0, geo)
    y, conv_state = conv_fn(h)
    o = _mm(y, w["w_out"], F32, geo)
    x, h = _epi(x, o, g_norm[0, 1], (mod0, 2), [(g_norm[0, 2], mod0, 4, 3)], geo)
    f = _mlp(h, w["wg_d"], w["wu_d"], w["wd_d"], geo)
    x, hq, hkv = _epi(x, f, g_norm[0, 3], (mod0, 5),
                      [(g_norm[1, 0], mod1, 1, 0), (w["g_kv"], modkv, 1, 0)], geo)
    k = _mm(hkv, w["wk"], F32, geo)
    v = _mm(hkv, w["wv"], F32, geo)
    lf = _logf(hkv, w["wf"], w["bf"], geo)[:, :N_HEADS]
    q = _mm(hq, w["wq"], attend_fn.q_dtype, geo)
    att = attend_fn(q, k, v, lf)
    o = _mm(att, w["wo"], F32, geo)
    x, h, gates = _epi(x, o, g_norm[1, 1], (mod1, 2), [(g_norm[1, 2], mod1, 4, 3)], geo,
                       router=(w["wr"], w["br"]))
    f = _moe(h, gates, w["wg_e"], w["wu_e"], w["wd_e"], geo)
    (x,) = _epi(x, f, g_norm[1, 3], (mod1, 5), [], geo)
    return x, conv_state, k, v, lf


def kernel(x_prompt, x_sample, state_conv, cache_k, cache_v, cache_logf, page_table, c_prompt, c_sample,
           w_mod, b_mod, g_norm, w_in_a, conv_w, w_out_a, g_kv, w_mod_kv, b_mod_kv, w_kv, b_f,
           w_q, w_o, w_gate_d, w_up_d, w_down_d, w_router, b_router, w_gate_e, w_up_e, w_down_e):
    nb_p, t_p, _ = x_prompt.shape
    nb_s = x_sample.shape[0]
    pad_s = SAMPLE_ROWS - nb_s
    n_phys = cache_k.shape[0]

    w = {
        "g_norm": g_norm, "g_kv": g_kv,
        "w_out": w_out_a[0].astype(BF),
        "wk": w_kv[:, :D].astype(BF), "wv": w_kv[:, D:2 * D].astype(BF),
        "wf": jnp.pad(w_kv[:, 2 * D:], ((0, 0), (0, LANES - N_HEADS))).astype(BF),
        "bf": jnp.pad(b_f, (0, LANES - N_HEADS)).reshape(1, LANES),
        "wq": w_q[0].astype(BF), "wo": w_o[0].astype(BF),
        "wg_d": w_gate_d[0].astype(BF), "wu_d": w_up_d[0].astype(BF), "wd_d": w_down_d[0].astype(BF),
        "wr": jnp.pad(w_router[0], ((0, 0), (0, LANES - N_EXPERTS))),
        "br": jnp.pad(b_router[0], (0, LANES - N_EXPERTS)).reshape(1, LANES),
        "wg_e": w_gate_e[0].astype(BF), "wu_e": w_up_e[0].astype(BF), "wd_e": w_down_e[0].astype(BF),
    }
    w_in = w_in_a[0].astype(BF)
    cw = conv_w[0]

    c16 = jnp.concatenate([c_prompt, c_sample, jnp.zeros((SAMPLE_ROWS - nb_p - nb_s, D), F32)], axis=0)
    mod = _modvec(c16, w_mod, b_mod)
    modkv = _modvec(c16, w_mod_kv[None], b_mod_kv[None])[0]

    def split(m):
        m_p = m[:nb_p][:, None, :]
        m_s = jnp.pad(m[nb_p:nb_p + nb_s], ((0, pad_s), (0, 0)))[None]
        return m_p, m_s

    mod0_p, mod0_s = split(mod[0])
    mod1_p, mod1_s = split(mod[1])
    modkv_p, modkv_s = split(modkv)

    geo_p = Geo(nb_p * t_p, t_p, 1)
    zeros_prev = jnp.zeros((nb_p, 2, D), F32)

    def attend_p(q, k, v, lf):
        tq = 512
        lf_t = jnp.transpose(lf.reshape(nb_p, t_p, N_HEADS), (0, 2, 1))
        fk = _cumsum(lf_t).reshape(nb_p * N_HEADS, t_p // tq, 1, tq)
        return _flash(q, k, v, fk, nb_p, t_p)

    attend_p.q_dtype = BF
    y_p, conv_p, k_p, v_p, lf_p = _trunk(
        x_prompt.reshape(nb_p * t_p, D), geo_p, mod0_p, mod1_p, modkv_p, w,
        lambda h: _conv_in(h, w_in, cw, zeros_prev, geo_p), attend_p)

    geo_s = Geo(SAMPLE_ROWS, SAMPLE_ROWS, SAMPLE_ROWS)
    pad_rows = lambda a: jnp.pad(a, ((0, pad_s), (0, 0)))
    prev0 = pad_rows(state_conv[0, :, 0])
    prev1 = pad_rows(state_conv[0, :, 1])
    ck = cache_k.reshape(n_phys, PAGE, D)
    cv = cache_v.reshape(n_phys, PAGE, D)
    clf_t = jnp.transpose(cache_logf, (0, 2, 1))

    def attend_s(q, k, v, lf):
        o = _decode(page_table, q[:nb_s, None, :], k[:nb_s, None, :], v[:nb_s, None, :],
                    lf[:nb_s, :, None], ck, cv, clf_t)
        return pad_rows(o[:, 0, :])

    attend_s.q_dtype = F32
    y_s, z_s, k_s, v_s, lf_s = _trunk(
        pad_rows(x_sample[:, 0, :]), geo_s, mod0_s, mod1_s, modkv_s, w,
        lambda h: _conv_in_step(h, w_in, cw, prev0, prev1, geo_s), attend_s)

    conv_s = jnp.stack([state_conv[0, :, 1], z_s[:nb_s]], axis=1)[None]
    return (y_p.reshape(nb_p, t_p, D),
            y_s[:nb_s].reshape(nb_s, 1, D),
            conv_p[None],
            conv_s,
            k_p.reshape(nb_p, t_p, N_HEADS, HEAD_DIM),
            v_p.reshape(nb_p, t_p, N_HEADS, HEAD_DIM),
            lf_p.reshape(nb_p, t_p, N_HEADS),
            k_s[:nb_s].reshape(nb_s, 1, N_HEADS, HEAD_DIM),
            v_s[:nb_s].reshape(nb_s, 1, N_HEADS, HEAD_DIM),
            lf_s[:nb_s].reshape(nb_s, 1, N_HEADS))
```

```python
import functools

import jax
import jax.numpy as jnp
from jax import lax
from jax.experimental import pallas as pl
from jax.experimental.pallas import tpu as pltpu

D = 2048
N_HEADS = 16
HEAD_DIM = 128
D_FF = 5632
N_EXPERTS = 8
EXPERT_FF = 1408
PAGE = 128
EPS = 1e-6
ATTN_SCALE = HEAD_DIM ** -0.5
LOG2E = 1.4426950408889634
LANES = 128
SAMPLE_ROWS = 16
VMEM_LIMIT = 56 * 1024 * 1024

BF = jnp.bfloat16
F32 = jnp.float32
NT_DIMS = (((1,), (1,)), ((), ()))


def _cp(*sem):
    return pltpu.CompilerParams(dimension_semantics=sem, vmem_limit_bytes=VMEM_LIMIT)


def _dot(a, b):
    return jnp.dot(a, b, preferred_element_type=F32)


class Geo:
    def __init__(self, m, rows_per_batch, mod_rows):
        self.m = m
        self.tb = rows_per_batch
        self.r = mod_rows

    def tile(self, pref):
        return min(pref, self.m)

    def mod_spec(self, tm, k):
        tpb = self.tb // tm
        return pl.BlockSpec((1, self.r, D), lambda i, *_: (i // tpb, 0, k))


def _vec_spec():
    return pl.BlockSpec((1, D), lambda i, *_: (0, 0))


def _modvec_kernel(c_ref, w_ref, b_ref, o_ref):
    o_ref[...] = _dot(c_ref[...].astype(BF), w_ref[...].astype(BF)) + b_ref[...]


def _modvec(c16, w, b):
    nl, _, n = w.shape
    tn = 1024
    return pl.pallas_call(
        _modvec_kernel,
        grid=(nl, n // tn),
        in_specs=[pl.BlockSpec((SAMPLE_ROWS, D), lambda l, j: (0, 0)),
                  pl.BlockSpec((None, D, tn), lambda l, j: (l, 0, j)),
                  pl.BlockSpec((None, 1, tn), lambda l, j: (l, 0, j))],
        out_specs=pl.BlockSpec((None, SAMPLE_ROWS, tn), lambda l, j: (l, 0, j)),
        out_shape=jax.ShapeDtypeStruct((nl, SAMPLE_ROWS, n), F32),
        compiler_params=_cp("arbitrary", "arbitrary"),
        name="modvec",
    )(c16, w, b.reshape(nl, 1, n))


def _norm_mod_kernel(x_ref, g_ref, sc_ref, sh_ref, h_ref):
    x = x_ref[...]
    r = lax.rsqrt(jnp.mean(x * x, axis=-1, keepdims=True) + EPS)
    h = (x * r * g_ref[...]) * (1.0 + sc_ref[0]) + sh_ref[0]
    h_ref[...] = h.astype(BF)


def _norm_mod(x, g, mod, sc_k, sh_k, geo):
    tm = geo.tile(512)
    row = pl.BlockSpec((tm, D), lambda i: (i, 0))
    return pl.pallas_call(
        _norm_mod_kernel,
        grid=(geo.m // tm,),
        in_specs=[row, _vec_spec(), geo.mod_spec(tm, sc_k), geo.mod_spec(tm, sh_k)],
        out_specs=row,
        out_shape=jax.ShapeDtypeStruct((geo.m, D), BF),
        compiler_params=_cp("arbitrary"),
        name="norm_mod",
    )(x, g.reshape(1, D), mod, mod)


def _top2_gates(h, wr, br):
    logits = jnp.dot(h, wr, preferred_element_type=F32, precision=lax.Precision.HIGHEST) + br
    lane = lax.broadcasted_iota(jnp.int32, logits.shape, 1)
    valid = lane < N_EXPERTS
    lg = jnp.where(valid, logits, -jnp.inf)
    e = jnp.exp(lg - jnp.max(lg, axis=-1, keepdims=True))
    p = e / jnp.sum(e, axis=-1, keepdims=True)
    p = jnp.where(valid, p, -1.0)
    v1 = jnp.max(p, axis=-1, keepdims=True)
    i1 = jnp.min(jnp.where(p == v1, lane, LANES), axis=-1, keepdims=True)
    p2 = jnp.where(lane == i1, -1.0, p)
    v2 = jnp.max(p2, axis=-1, keepdims=True)
    i2 = jnp.min(jnp.where(p2 == v2, lane, LANES), axis=-1, keepdims=True)
    den = v1 + v2
    return jnp.where(lane == i1, v1 / den, 0.0) + jnp.where(lane == i2, v2 / den, 0.0)


def _epi_kernel(*refs, n_pro, router):
    x_ref, f_ref, gpost_ref, gate_ref = refs[:4]
    pos = 4
    pro = [refs[pos + 3 * k: pos + 3 * k + 3] for k in range(n_pro)]
    pos += 3 * n_pro
    if router:
        wr_ref, br_ref = refs[pos:pos + 2]
        pos += 2
    xo_ref = refs[pos]
    h_refs = refs[pos + 1: pos + 1 + n_pro]
    f = f_ref[...]
    rf = lax.rsqrt(jnp.mean(f * f, axis=-1, keepdims=True) + EPS)
    xn = x_ref[...] + gate_ref[0] * (f * rf * gpost_ref[...])
    xo_ref[...] = xn
    if n_pro:
        xr = xn * lax.rsqrt(jnp.mean(xn * xn, axis=-1, keepdims=True) + EPS)
        for k, (g_ref, sc_ref, sh_ref) in enumerate(pro):
            h = (xr * g_ref[...]) * (1.0 + sc_ref[0]) + sh_ref[0]
            h_refs[k][...] = h.astype(BF)
            if router and k == 0:
                refs[pos + 1 + n_pro][...] = _top2_gates(h, wr_ref[...], br_ref[...])


def _epi(x, f, g_post, gate, pro, geo, router=None):
    mod_gate, gate_k = gate
    tm = geo.tile(256)
    row = pl.BlockSpec((tm, D), lambda i: (i, 0))
    in_specs = [row, row, _vec_spec(), geo.mod_spec(tm, gate_k)]
    args = [x, f, g_post.reshape(1, D), mod_gate]
    for g, mod, sc_k, sh_k in pro:
        in_specs += [_vec_spec(), geo.mod_spec(tm, sc_k), geo.mod_spec(tm, sh_k)]
        args += [g.reshape(1, D), mod, mod]
    out_specs = [row] + [row] * len(pro)
    out_shape = [jax.ShapeDtypeStruct((geo.m, D), F32)] + [jax.ShapeDtypeStruct((geo.m, D), BF)] * len(pro)
    if router is not None:
        wr, br = router
        in_specs += [pl.BlockSpec((D, LANES), lambda i: (0, 0)), pl.BlockSpec((1, LANES), lambda i: (0, 0))]
        args += [wr, br]
        out_specs.append(pl.BlockSpec((tm, LANES), lambda i: (i, 0)))
        out_shape.append(jax.ShapeDtypeStruct((geo.m, LANES), F32))
    return pl.pallas_call(
        functools.partial(_epi_kernel, n_pro=len(pro), router=router is not None),
        grid=(geo.m // tm,),
        in_specs=in_specs,
        out_specs=out_specs,
        out_shape=out_shape,
        compiler_params=_cp("arbitrary"),
        name="epi",
    )(*args)


def _mm_kernel(a_ref, w_ref, o_ref, *, scale):
    acc = _dot(a_ref[...], w_ref[...])
    if scale != 1.0:
        acc = acc * scale
    o_ref[...] = acc.astype(o_ref.dtype)


def _mm(a, w, out_dtype, geo, scale=1.0):
    n = w.shape[1]
    tm, tn = geo.tile(1024), min(1024, n)
    return pl.pallas_call(
        functools.partial(_mm_kernel, scale=scale),
        grid=(geo.m // tm, n // tn),
        in_specs=[pl.BlockSpec((tm, D), lambda i, j: (i, 0)), pl.BlockSpec((D, tn), lambda i, j: (0, j))],
        out_specs=pl.BlockSpec((tm, tn), lambda i, j: (i, j)),
        out_shape=jax.ShapeDtypeStruct((geo.m, n), out_dtype),
        compiler_params=_cp("arbitrary", "arbitrary"),
        name="mm",
    )(a, w)


def _logf_kernel(a_ref, w_ref, b_ref, o_ref):
    f = _dot(a_ref[...], w_ref[...]) + b_ref[...]
    o_ref[...] = jnp.minimum(f, 0.0) - jnp.log1p(jnp.exp(-jnp.abs(f)))


def _logf(a, w, b, geo):
    tm = geo.tile(1024)
    return pl.pallas_call(
        _logf_kernel,
        grid=(geo.m // tm,),
        in_specs=[pl.BlockSpec((tm, D), lambda i: (i, 0)), pl.BlockSpec((D, LANES), lambda i: (0, 0)),
                  pl.BlockSpec((1, LANES), lambda i: (0, 0))],
        out_specs=pl.BlockSpec((tm, LANES), lambda i: (i, 0)),
        out_shape=jax.ShapeDtypeStruct((geo.m, LANES), F32),
        compiler_params=_cp("arbitrary"),
        name="logf",
    )(a, w, b)


def _conv_in_kernel(a_ref, wb_ref, wc_ref, wu_ref, cw_ref, prev_ref, y_ref, st_ref, carry_ref, *, tpb):
    i, j = pl.program_id(0), pl.program_id(1)
    a = a_ref[...]
    bg = _dot(a, wb_ref[...])
    z = _dot(a, wc_ref[...]) * _dot(a, wu_ref[...])
    tm = z.shape[0]

    @pl.when(i % tpb == 0)
    def _():
        carry_ref[j] = prev_ref[0]

    prev = carry_ref[j]
    p0, p1 = prev[0:1], prev[1:2]
    row = lax.broadcasted_iota(jnp.int32, z.shape, 0)
    z1 = jnp.where(row == 0, p1, pltpu.roll(z, 1, 0))
    z2 = jnp.where(row == 0, p0, jnp.where(row == 1, p1, pltpu.roll(z, 2, 0)))
    cw = cw_ref[...]
    y = z2 * cw[0:1] + z1 * cw[1:2] + z * cw[2:3]
    y_ref[...] = (bg * y).astype(BF)
    last = z[tm - 2:tm]
    carry_ref[j] = last
    st_ref[0] = last


def _conv_in(h, w_in, conv_w, prev, geo):
    tm, tn = geo.tile(1024), 512
    nj = D // tn
    tpb = geo.tb // tm
    wspec = lambda off: pl.BlockSpec((D, tn), lambda i, j: (0, off * nj + j))
    y, tails = pl.pallas_call(
        functools.partial(_conv_in_kernel, tpb=tpb),
        grid=(geo.m // tm, nj),
        in_specs=[pl.BlockSpec((tm, D), lambda i, j: (i, 0)), wspec(0), wspec(1), wspec(2),
                  pl.BlockSpec((3, tn), lambda i, j: (0, j)),
                  pl.BlockSpec((1, 2, tn), lambda i, j: (i // tpb, 0, j))],
        out_specs=[pl.BlockSpec((tm, tn), lambda i, j: (i, j)),
                   pl.BlockSpec((1, 2, tn), lambda i, j: (i, 0, j))],
        out_shape=[jax.ShapeDtypeStruct((geo.m, D), BF),
                   jax.ShapeDtypeStruct((geo.m // tm, 2, D), F32)],
        scratch_shapes=[pltpu.VMEM((nj, 2, tn), F32)],
        compiler_params=_cp("arbitrary", "arbitrary"),
        name="conv_in",
    )(h, w_in, w_in, w_in, conv_w, prev)
    return y, tails[tpb - 1::tpb]


def _conv_in_step_kernel(a_ref, wb_ref, wc_ref, wu_ref, cw_ref, p0_ref, p1_ref, y_ref, z_ref):
    a = a_ref[...]
    bg = _dot(a, wb_ref[...])
    z = _dot(a, wc_ref[...]) * _dot(a, wu_ref[...])
    cw = cw_ref[...]
    y = p0_ref[...] * cw[0:1] + p1_ref[...] * cw[1:2] + z * cw[2:3]
    y_ref[...] = (bg * y).astype(BF)
    z_ref[...] = z


def _conv_in_step(h, w_in, conv_w, p0, p1, geo):
    tm, tn = geo.m, 512
    nj = D // tn
    wspec = lambda off: pl.BlockSpec((D, tn), lambda j: (0, off * nj + j))
    blk = pl.BlockSpec((tm, tn), lambda j: (0, j))
    return pl.pallas_call(
        _conv_in_step_kernel,
        grid=(nj,),
        in_specs=[pl.BlockSpec((tm, D), lambda j: (0, 0)), wspec(0), wspec(1), wspec(2),
                  pl.BlockSpec((3, tn), lambda j: (0, j)), blk, blk],
        out_specs=[blk, blk],
        out_shape=[jax.ShapeDtypeStruct((tm, D), BF), jax.ShapeDtypeStruct((tm, D), F32)],
        compiler_params=_cp("arbitrary"),
        name="conv_in_step",
    )(h, w_in, w_in, w_in, conv_w, p0, p1)


def _mlp_kernel(a_ref, wg_ref, wu_ref, wd_ref, *rest, gated):
    o_ref = rest[-1]
    f = pl.program_id(1)
    a = a_ref[...]
    g = _dot(a, wg_ref[...])
    u = _dot(a, wu_ref[...])
    hid = (g * (1.0 / (1.0 + jnp.exp(-g)))) * u
    if gated:
        gates = rest[0][...]
        lane = lax.broadcasted_iota(jnp.int32, gates.shape, 1)
        hid = hid * jnp.sum(jnp.where(lane == f, gates, 0.0), axis=-1, keepdims=True)
    part = _dot(hid.astype(BF), wd_ref[...])

    @pl.when(f == 0)
    def _():
        o_ref[...] = part

    @pl.when(f > 0)
    def _():
        o_ref[...] += part


def _mlp(h, wg, wu, wd, geo):
    tm, tf = geo.tile(1024), 512
    return pl.pallas_call(
        functools.partial(_mlp_kernel, gated=False),
        grid=(geo.m // tm, D_FF // tf),
        in_specs=[pl.BlockSpec((tm, D), lambda i, f: (i, 0)),
                  pl.BlockSpec((D, tf), lambda i, f: (0, f)),
                  pl.BlockSpec((D, tf), lambda i, f: (0, f)),
                  pl.BlockSpec((tf, D), lambda i, f: (f, 0))],
        out_specs=pl.BlockSpec((tm, D), lambda i, f: (i, 0)),
        out_shape=jax.ShapeDtypeStruct((geo.m, D), F32),
        compiler_params=_cp("arbitrary", "arbitrary"),
        name="mlp",
    )(h, wg, wu, wd)


def _moe(h, gates, wg, wu, wd, geo):
    tm = geo.tile(256)
    return pl.pallas_call(
        functools.partial(_mlp_kernel, gated=True),
        grid=(geo.m // tm, N_EXPERTS),
        in_specs=[pl.BlockSpec((tm, D), lambda i, e: (i, 0)),
                  pl.BlockSpec((None, D, EXPERT_FF), lambda i, e: (e, 0, 0)),
                  pl.BlockSpec((None, D, EXPERT_FF), lambda i, e: (e, 0, 0)),
                  pl.BlockSpec((None, EXPERT_FF, D), lambda i, e: (e, 0, 0)),
                  pl.BlockSpec((tm, LANES), lambda i, e: (i, 0))],
        out_specs=pl.BlockSpec((tm, D), lambda i, e: (i, 0)),
        out_shape=jax.ShapeDtypeStruct((geo.m, D), F32),
        compiler_params=_cp("arbitrary", "arbitrary"),
        name="moe",
    )(h, wg, wu, wd, gates)


def _cumsum_lanes(x):
    n = x.shape[-1]
    lane = lax.broadcasted_iota(jnp.int32, x.shape, x.ndim - 1)
    s = 1
    while s < n:
        x = x + jnp.where(lane >= s, pltpu.roll(x, s, x.ndim - 1), 0.0)
        s *= 2
    return x


def _cumsum_kernel(x_ref, o_ref, *, scale):
    o_ref[0] = _cumsum_lanes(x_ref[0]) * scale


def _cumsum(lf_t, scale):
    b, h, t = lf_t.shape
    spec = pl.BlockSpec((1, h, t), lambda i: (i, 0, 0))
    return pl.pallas_call(
        functools.partial(_cumsum_kernel, scale=scale), grid=(b,), in_specs=[spec], out_specs=spec,
        out_shape=jax.ShapeDtypeStruct(lf_t.shape, F32),
        compiler_params=_cp("arbitrary"), name="cumsum",
    )(lf_t)


def _flash_kernel(q_ref, k_ref, v_ref, fk_ref, o_ref, kb_ref, vb_ref, *, tq, tk):
    qi = pl.program_id(2)

    @pl.when(qi == 0)
    def _():
        kb_ref[...] = k_ref[...].astype(BF)
        vb_ref[...] = v_ref[...].astype(BF)

    q = q_ref[...]

    def step(kj, carry, masked):
        m, l, acc = carry
        start = pl.multiple_of(kj * tk, tk)
        k = kb_ref[pl.ds(start, tk), :]
        v = vb_ref[pl.ds(start, tk), :]
        s = lax.dot_general(q, k, NT_DIMS, preferred_element_type=F32) - fk_ref[0, kj]
        if masked:
            row = lax.broadcasted_iota(jnp.int32, s.shape, 0) + qi * tq
            col = lax.broadcasted_iota(jnp.int32, s.shape, 1) + kj * tk
            s = jnp.where(col <= row, s, -jnp.inf)
        m_new = jnp.maximum(m, jnp.max(s, axis=-1, keepdims=True))
        alpha = jnp.exp2(m - m_new)
        p = jnp.exp2(s - m_new)
        l = alpha * l + jnp.sum(p, axis=-1, keepdims=True)
        acc = alpha * acc + _dot(p.astype(BF), v)
        return m_new, l, acc

    init = (jnp.full((tq, 1), -jnp.inf, F32), jnp.zeros((tq, 1), F32), jnp.zeros((tq, HEAD_DIM), F32))
    n_full = (qi * tq) // tk
    carry = lax.fori_loop(0, n_full, lambda kj, c: step(kj, c, False), init)
    _, l, acc = step(n_full, carry, True)
    o_ref[...] = (acc / l).astype(o_ref.dtype)


FLASH_TQ = 512
FLASH_TK = 512


def _flash(q, k, v, fk, nb, t):
    tq, tk = FLASH_TQ, FLASH_TK
    assert tk % tq == 0
    nq = t // tq
    return pl.pallas_call(
        functools.partial(_flash_kernel, tq=tq, tk=tk),
        grid=(nb, N_HEADS, nq),
        in_specs=[pl.BlockSpec((tq, HEAD_DIM), lambda b, h, i: (b * nq + i, h)),
                  pl.BlockSpec((t, HEAD_DIM), lambda b, h, i: (b, h)),
                  pl.BlockSpec((t, HEAD_DIM), lambda b, h, i: (b, h)),
                  pl.BlockSpec((1, t // tk, 1, tk), lambda b, h, i: (b * N_HEADS + h, 0, 0, 0))],
        out_specs=pl.BlockSpec((tq, HEAD_DIM), lambda b, h, i: (b * nq + i, h)),
        out_shape=jax.ShapeDtypeStruct((nb * t, D), BF),
        scratch_shapes=[pltpu.VMEM((t, HEAD_DIM), BF), pltpu.VMEM((t, HEAD_DIM), BF)],
        compiler_params=_cp("arbitrary", "arbitrary", "arbitrary"),
        name="flash",
    )(q, k, v, fk)


PAGE_COLS = PAGE * N_HEADS
PAGES_PER_STEP = 4


def _head_cumsum(x):
    lane = lax.broadcasted_iota(jnp.int32, x.shape, 1)
    s = N_HEADS
    while s < PAGE_COLS:
        x = x + jnp.where(lane >= s, pltpu.roll(x, s, 1), 0.0)
        s *= 2
    return x


def _head_totals(x):
    lane = lax.broadcasted_iota(jnp.int32, x.shape, 1)
    y = jnp.where(lane >= PAGE_COLS - N_HEADS, x, 0.0)
    s = N_HEADS
    while s < PAGE_COLS:
        y = y + pltpu.roll(y, PAGE_COLS - s, 1)
        s *= 2
    return y


def _decode_kernel(pt_ref, q_ref, kn_ref, vn_ref, lfn_ref, *refs):
    n = PAGES_PER_STEP
    k_refs, v_refs, lf_refs = refs[:n], refs[n:2 * n], refs[2 * n:3 * n]
    o_ref, qb_ref, m_ref, l_ref, acc_ref, fc_ref, kb_ref, vb_ref = refs[3 * n:]
    p = pl.program_id(1)
    lane = lax.broadcasted_iota(jnp.int32, (N_HEADS, n * PAGE_COLS), 1)
    head = lax.broadcasted_iota(jnp.int32, (N_HEADS, n * PAGE_COLS), 0)
    own = lane % N_HEADS == head

    @pl.when(p == 0)
    def _():
        qb_ref[...] = q_ref[0].astype(BF)
        m_ref[...] = jnp.full(m_ref.shape, -jnp.inf, F32)
        l_ref[...] = jnp.zeros(l_ref.shape, F32)
        acc_ref[...] = jnp.zeros(acc_ref.shape, F32)
        fc_ref[...] = jnp.zeros(fc_ref.shape, F32)

    qb = qb_ref[...]
    for i in range(n):
        kb_ref[i * PAGE_COLS:(i + 1) * PAGE_COLS, :] = k_refs[i][0].astype(BF)
        vb_ref[i * PAGE_COLS:(i + 1) * PAGE_COLS, :] = v_refs[i][0].astype(BF)
    s = lax.dot_general(qb, kb_ref[...], NT_DIMS, preferred_element_type=F32)
    cs = _head_cumsum(jnp.concatenate([r[0] for r in lf_refs], axis=0))
    tot = _head_totals(cs)
    run = fc_ref[...]
    fps = []
    for i in range(n):
        fps.append(cs[i:i + 1] + run)
        run = run + tot[i:i + 1]
    fc = run
    s = jnp.where(own, s * ATTN_SCALE - jnp.concatenate(fps, axis=1), -jnp.inf)
    m_old = m_ref[...]
    m = jnp.maximum(m_old, jnp.max(s, axis=-1, keepdims=True))
    alpha = jnp.exp(m_old - m)
    pr = jnp.exp(s - m[:, 0:1])
    l = alpha * l_ref[...] + jnp.sum(pr, axis=-1, keepdims=True)
    acc = alpha * acc_ref[...] + _dot(pr.astype(BF), vb_ref[...])
    m_ref[...] = m
    l_ref[...] = l
    acc_ref[...] = acc
    fc_ref[...] = fc

    @pl.when(p == pl.num_programs(1) - 1)
    def _():
        kn = kn_ref[0].astype(BF).astype(F32)
        vn = vn_ref[0].astype(BF).astype(F32)
        on_head = (lax.broadcasted_iota(jnp.int32, (N_HEADS, PAGE_COLS), 1)
                   == lax.broadcasted_iota(jnp.int32, (N_HEADS, PAGE_COLS), 0))
        f_past = jnp.sum(jnp.where(on_head, fc, 0.0), axis=-1, keepdims=True)
        s_n = jnp.sum(qb.astype(F32) * kn, axis=-1, keepdims=True) * ATTN_SCALE
        s_n = s_n - (f_past + lfn_ref[0])
        m1 = m[:, 0:1]
        m2 = jnp.maximum(m1, s_n)
        a2 = jnp.exp(m1 - m2)
        pn = jnp.exp(s_n - m2)
        l2 = a2 * l[:, 0:1] + pn
        acc2 = a2 * acc + pn.astype(BF).astype(F32) * vn
        o_ref[0] = (acc2 / l2).astype(o_ref.dtype)


def _decode(page_table, q, kn, vn, lfn, cache_k, cache_v, cache_lf):
    nb, n_pages = page_table.shape
    n = PAGES_PER_STEP
    tok = pl.BlockSpec((1, N_HEADS, HEAD_DIM), lambda b, p, pt: (b, 0, 0))
    page = lambda i: pl.BlockSpec((1, PAGE_COLS, HEAD_DIM), lambda b, p, pt: (pt[b, n * p + i], 0, 0))
    lfpage = lambda i: pl.BlockSpec((1, 1, PAGE_COLS), lambda b, p, pt: (pt[b, n * p + i], 0, 0))
    grid_spec = pltpu.PrefetchScalarGridSpec(
        num_scalar_prefetch=1,
        grid=(nb, n_pages // n),
        in_specs=[tok, tok, tok, pl.BlockSpec((1, N_HEADS, 1), lambda b, p, pt: (b, 0, 0))]
        + [page(i) for i in range(n)] + [page(i) for i in range(n)] + [lfpage(i) for i in range(n)],
        out_specs=tok,
        scratch_shapes=[pltpu.VMEM((N_HEADS, HEAD_DIM), BF),
                        pltpu.VMEM((N_HEADS, LANES), F32),
                        pltpu.VMEM((N_HEADS, LANES), F32),
                        pltpu.VMEM((N_HEADS, HEAD_DIM), F32),
                        pltpu.VMEM((1, PAGE_COLS), F32),
                        pltpu.VMEM((n * PAGE_COLS, HEAD_DIM), BF),
                        pltpu.VMEM((n * PAGE_COLS, HEAD_DIM), BF)],
    )
    return pl.pallas_call(
        _decode_kernel,
        grid_spec=grid_spec,
        out_shape=jax.ShapeDtypeStruct((nb, N_HEADS, HEAD_DIM), BF),
        compiler_params=_cp("arbitrary", "arbitrary"),
        name="decode",
    )(page_table, q, kn, vn, lfn, *([cache_k] * n), *([cache_v] * n), *([cache_lf] * n))


def _trunk(x, geo, mod0, mod1, modkv, w, conv_fn, attend_fn):
    g_norm = w["g_norm"]
    h = _norm_mod(x, g_norm[0, 0], mod0, 1, 0, geo)
    y, conv_state = conv_fn(h)
    o = _mm(y, w["w_out"], F32, geo)
    x, h = _epi(x, o, g_norm[0, 1], (mod0, 2), [(g_norm[0, 2], mod0, 4, 3)], geo)
    f = _mlp(h, w["wg_d"], w["wu_d"], w["wd_d"], geo)
    x, hq, hkv = _epi(x, f, g_norm[0, 3], (mod0, 5),
                      [(g_norm[1, 0], mod1, 1, 0), (w["g_kv"], modkv, 1, 0)], geo)
    k = _mm(hkv, w["wk"], F32, geo)
    v = _mm(hkv, w["wv"], F32, geo)
    lf = _logf(hkv, w["wf"], w["bf"], geo)[:, :N_HEADS]
    q = _mm(hq, w["wq"], attend_fn.q_dtype, geo, scale=attend_fn.q_scale)
    att = attend_fn(q, k, v, lf)
    o = _mm(att, w["wo"], F32, geo)
    x, h, gates = _epi(x, o, g_norm[1, 1], (mod1, 2), [(g_norm[1, 2], mod1, 4, 3)], geo,
                       router=(w["wr"], w["br"]))
    f = _moe(h, gates, w["wg_e"], w["wu_e"], w["wd_e"], geo)
    (x,) = _epi(x, f, g_norm[1, 3], (mod1, 5), [], geo)
    return x, conv_state, k, v, lf


def kernel(x_prompt, x_sample, state_conv, cache_k, cache_v, cache_logf, page_table, c_prompt, c_sample,
           w_mod, b_mod, g_norm, w_in_a, conv_w, w_out_a, g_kv, w_mod_kv, b_mod_kv, w_kv, b_f,
           w_q, w_o, w_gate_d, w_up_d, w_down_d, w_router, b_router, w_gate_e, w_up_e, w_down_e):
    nb_p, t_p, _ = x_prompt.shape
    nb_s = x_sample.shape[0]
    pad_s = SAMPLE_ROWS - nb_s
    n_phys = cache_k.shape[0]

    w = {
        "g_norm": g_norm, "g_kv": g_kv,
        "w_out": w_out_a[0].astype(BF),
        "wk": w_kv[:, :D].astype(BF), "wv": w_kv[:, D:2 * D].astype(BF),
        "wf": jnp.pad(w_kv[:, 2 * D:], ((0, 0), (0, LANES - N_HEADS))).astype(BF),
        "bf": jnp.pad(b_f, (0, LANES - N_HEADS)).reshape(1, LANES),
        "wq": w_q[0].astype(BF), "wo": w_o[0].astype(BF),
        "wg_d": w_gate_d[0].astype(BF), "wu_d": w_up_d[0].astype(BF), "wd_d": w_down_d[0].astype(BF),
        "wr": jnp.pad(w_router[0], ((0, 0), (0, LANES - N_EXPERTS))),
        "br": jnp.pad(b_router[0], (0, LANES - N_EXPERTS)).reshape(1, LANES),
        "wg_e": w_gate_e[0].astype(BF), "wu_e": w_up_e[0].astype(BF), "wd_e": w_down_e[0].astype(BF),
    }
    w_in = w_in_a[0].astype(BF)
    cw = conv_w[0]

    c16 = jnp.concatenate([c_prompt, c_sample, jnp.zeros((SAMPLE_ROWS - nb_p - nb_s, D), F32)], axis=0)
    mod = _modvec(c16, w_mod, b_mod)
    modkv = _modvec(c16, w_mod_kv[None], b_mod_kv[None])[0]

    def split(m):
        m_p = m[:nb_p][:, None, :]
        m_s = jnp.pad(m[nb_p:nb_p + nb_s], ((0, pad_s), (0, 0)))[None]
        return m_p, m_s

    mod0_p, mod0_s = split(mod[0])
    mod1_p, mod1_s = split(mod[1])
    modkv_p, modkv_s = split(modkv)

    geo_p = Geo(nb_p * t_p, t_p, 1)
    zeros_prev = jnp.zeros((nb_p, 2, D), F32)

    def attend_p(q, k, v, lf):
        lf_t = jnp.transpose(lf.reshape(nb_p, t_p, N_HEADS), (0, 2, 1))
        fk = _cumsum(lf_t, LOG2E).reshape(nb_p * N_HEADS, t_p // FLASH_TK, 1, FLASH_TK)
        return _flash(q, k, v, fk, nb_p, t_p)

    attend_p.q_dtype, attend_p.q_scale = BF, ATTN_SCALE * LOG2E
    y_p, conv_p, k_p, v_p, lf_p = _trunk(
        x_prompt.reshape(nb_p * t_p, D), geo_p, mod0_p, mod1_p, modkv_p, w,
        lambda h: _conv_in(h, w_in, cw, zeros_prev, geo_p), attend_p)

    geo_s = Geo(SAMPLE_ROWS, SAMPLE_ROWS, SAMPLE_ROWS)
    pad_rows = lambda a: jnp.pad(a, ((0, pad_s), (0, 0)))
    prev0 = pad_rows(state_conv[0, :, 0])
    prev1 = pad_rows(state_conv[0, :, 1])
    ck = cache_k.reshape(n_phys, PAGE_COLS, HEAD_DIM)
    cv = cache_v.reshape(n_phys, PAGE_COLS, HEAD_DIM)
    clf = cache_logf.reshape(n_phys, 1, PAGE_COLS)
    heads = lambda a: a[:nb_s].reshape(nb_s, N_HEADS, HEAD_DIM)

    def attend_s(q, k, v, lf):
        o = _decode(page_table, heads(q), heads(k), heads(v), lf[:nb_s, :, None], ck, cv, clf)
        return pad_rows(o.reshape(nb_s, D))

    attend_s.q_dtype, attend_s.q_scale = F32, 1.0
    y_s, z_s, k_s, v_s, lf_s = _trunk(
        pad_rows(x_sample[:, 0, :]), geo_s, mod0_s, mod1_s, modkv_s, w,
        lambda h: _conv_in_step(h, w_in, cw, prev0, prev1, geo_s), attend_s)

    conv_s = jnp.stack([state_conv[0, :, 1], z_s[:nb_s]], axis=1)[None]
    return (y_p.reshape(nb_p, t_p, D),
            y_s[:nb_s].reshape(nb_s, 1, D),
            conv_p[None],
            conv_s,
            k_p.reshape(nb_p, t_p, N_HEADS, HEAD_DIM),
            v_p.reshape(nb_p, t_p, N_HEADS, HEAD_DIM),
            lf_p.reshape(nb_p, t_p, N_HEADS),
            k_s[:nb_s].reshape(nb_s, 1, N_HEADS, HEAD_DIM),
            v_s[:nb_s].reshape(nb_s, 1, N_HEADS, HEAD_DIM),
            lf_s[:nb_s].reshape(nb_s, 1, N_HEADS))
```

```python
import functools

import jax
import jax.numpy as jnp
from jax import lax
from jax.experimental import pallas as pl
from jax.experimental.pallas import tpu as pltpu

D = 2048
N_HEADS = 16
HEAD_DIM = 128
D_FF = 5632
N_EXPERTS = 8
EXPERT_FF = 1408
PAGE = 128
EPS = 1e-6
ATTN_SCALE = HEAD_DIM ** -0.5
LOG2E = 1.4426950408889634
LANES = 128
SAMPLE_ROWS = 16
VMEM_LIMIT = 56 * 1024 * 1024

BF = jnp.bfloat16
F32 = jnp.float32
NT_DIMS = (((1,), (1,)), ((), ()))


def _cp(*sem):
    return pltpu.CompilerParams(dimension_semantics=sem, vmem_limit_bytes=VMEM_LIMIT)


def _dot(a, b):
    return jnp.dot(a, b, preferred_element_type=F32)


class Geo:
    def __init__(self, m, rows_per_batch, mod_rows):
        self.m = m
        self.tb = rows_per_batch
        self.r = mod_rows

    def tile(self, pref):
        return min(pref, self.m)

    def mod_spec(self, tm, k):
        tpb = self.tb // tm
        return pl.BlockSpec((1, self.r, D), lambda i, *_: (i // tpb, 0, k))


def _vec_spec():
    return pl.BlockSpec((1, D), lambda i, *_: (0, 0))


def _modvec_kernel(c_ref, w_ref, b_ref, o_ref):
    o_ref[...] = _dot(c_ref[...].astype(BF), w_ref[...].astype(BF)) + b_ref[...]


def _modvec(c16, w, b):
    nl, _, n = w.shape
    tn = 1024
    return pl.pallas_call(
        _modvec_kernel,
        grid=(nl, n // tn),
        in_specs=[pl.BlockSpec((SAMPLE_ROWS, D), lambda l, j: (0, 0)),
                  pl.BlockSpec((None, D, tn), lambda l, j: (l, 0, j)),
                  pl.BlockSpec((None, 1, tn), lambda l, j: (l, 0, j))],
        out_specs=pl.BlockSpec((None, SAMPLE_ROWS, tn), lambda l, j: (l, 0, j)),
        out_shape=jax.ShapeDtypeStruct((nl, SAMPLE_ROWS, n), F32),
        compiler_params=_cp("arbitrary", "arbitrary"),
        name="modvec",
    )(c16, w, b.reshape(nl, 1, n))


def _norm_mod_kernel(x_ref, g_ref, sc_ref, sh_ref, h_ref):
    x = x_ref[...]
    r = lax.rsqrt(jnp.mean(x * x, axis=-1, keepdims=True) + EPS)
    h = (x * r * g_ref[...]) * (1.0 + sc_ref[0]) + sh_ref[0]
    h_ref[...] = h.astype(BF)


def _norm_mod(x, g, mod, sc_k, sh_k, geo):
    tm = geo.tile(512)
    row = pl.BlockSpec((tm, D), lambda i: (i, 0))
    return pl.pallas_call(
        _norm_mod_kernel,
        grid=(geo.m // tm,),
        in_specs=[row, _vec_spec(), geo.mod_spec(tm, sc_k), geo.mod_spec(tm, sh_k)],
        out_specs=row,
        out_shape=jax.ShapeDtypeStruct((geo.m, D), BF),
        compiler_params=_cp("arbitrary"),
        name="norm_mod",
    )(x, g.reshape(1, D), mod, mod)


def _top2_gates(h, wr, br):
    logits = jnp.dot(h, wr, preferred_element_type=F32, precision=lax.Precision.HIGHEST) + br
    lane = lax.broadcasted_iota(jnp.int32, logits.shape, 1)
    valid = lane < N_EXPERTS
    lg = jnp.where(valid, logits, -jnp.inf)
    e = jnp.exp(lg - jnp.max(lg, axis=-1, keepdims=True))
    p = e / jnp.sum(e, axis=-1, keepdims=True)
    p = jnp.where(valid, p, -1.0)
    v1 = jnp.max(p, axis=-1, keepdims=True)
    i1 = jnp.min(jnp.where(p == v1, lane, LANES), axis=-1, keepdims=True)
    p2 = jnp.where(lane == i1, -1.0, p)
    v2 = jnp.max(p2, axis=-1, keepdims=True)
    i2 = jnp.min(jnp.where(p2 == v2, lane, LANES), axis=-1, keepdims=True)
    den = v1 + v2
    gates = jnp.where(lane == i1, v1 / den, 0.0) + jnp.where(lane == i2, v2 / den, 0.0)
    chosen = jnp.where((lane == i1) | (lane == i2), 1.0, 0.0)
    return gates, chosen


def _epi_kernel(*refs, n_pro, router):
    x_ref, f_ref, gpost_ref, gate_ref = refs[:4]
    pos = 4
    pro = [refs[pos + 3 * k: pos + 3 * k + 3] for k in range(n_pro)]
    pos += 3 * n_pro
    if router:
        wr_ref, br_ref = refs[pos:pos + 2]
        pos += 2
    xo_ref = refs[pos]
    h_refs = refs[pos + 1: pos + 1 + n_pro]
    f = f_ref[...]
    rf = lax.rsqrt(jnp.mean(f * f, axis=-1, keepdims=True) + EPS)
    xn = x_ref[...] + gate_ref[0] * (f * rf * gpost_ref[...])
    xo_ref[...] = xn
    if n_pro:
        xr = xn * lax.rsqrt(jnp.mean(xn * xn, axis=-1, keepdims=True) + EPS)
        for k, (g_ref, sc_ref, sh_ref) in enumerate(pro):
            h = (xr * g_ref[...]) * (1.0 + sc_ref[0]) + sh_ref[0]
            h_refs[k][...] = h.astype(h_refs[k].dtype)
            if router and k == 0:
                gates, chosen = _top2_gates(h, wr_ref[...], br_ref[...])
                refs[pos + 1 + n_pro][...] = gates
                refs[pos + 2 + n_pro][...] = chosen


def _epi(x, f, g_post, gate, pro, geo, router=None, h_dtype=BF):
    mod_gate, gate_k = gate
    tm = geo.tile(256)
    row = pl.BlockSpec((tm, D), lambda i: (i, 0))
    in_specs = [row, row, _vec_spec(), geo.mod_spec(tm, gate_k)]
    args = [x, f, g_post.reshape(1, D), mod_gate]
    for g, mod, sc_k, sh_k in pro:
        in_specs += [_vec_spec(), geo.mod_spec(tm, sc_k), geo.mod_spec(tm, sh_k)]
        args += [g.reshape(1, D), mod, mod]
    out_specs = [row] + [row] * len(pro)
    out_shape = [jax.ShapeDtypeStruct((geo.m, D), F32)] + [jax.ShapeDtypeStruct((geo.m, D), h_dtype)] * len(pro)
    if router is not None:
        wr, br = router
        in_specs += [pl.BlockSpec((D, LANES), lambda i: (0, 0)), pl.BlockSpec((1, LANES), lambda i: (0, 0))]
        args += [wr, br]
        out_specs += [pl.BlockSpec((tm, LANES), lambda i: (i, 0))] * 2
        out_shape += [jax.ShapeDtypeStruct((geo.m, LANES), F32)] * 2
    return pl.pallas_call(
        functools.partial(_epi_kernel, n_pro=len(pro), router=router is not None),
        grid=(geo.m // tm,),
        in_specs=in_specs,
        out_specs=out_specs,
        out_shape=out_shape,
        compiler_params=_cp("arbitrary"),
        name="epi",
    )(*args)


def _mm_kernel(a_ref, w_ref, o_ref, *, scale):
    acc = _dot(a_ref[...], w_ref[...])
    if scale != 1.0:
        acc = acc * scale
    o_ref[...] = acc.astype(o_ref.dtype)


def _mm(a, w, out_dtype, geo, scale=1.0):
    n = w.shape[1]
    tm, tn = geo.tile(1024), min(1024, n)
    return pl.pallas_call(
        functools.partial(_mm_kernel, scale=scale),
        grid=(geo.m // tm, n // tn),
        in_specs=[pl.BlockSpec((tm, D), lambda i, j: (i, 0)), pl.BlockSpec((D, tn), lambda i, j: (0, j))],
        out_specs=pl.BlockSpec((tm, tn), lambda i, j: (i, j)),
        out_shape=jax.ShapeDtypeStruct((geo.m, n), out_dtype),
        compiler_params=_cp("arbitrary", "arbitrary"),
        name="mm",
    )(a, w)


def _logf_kernel(a_ref, w_ref, b_ref, o_ref):
    f = _dot(a_ref[...], w_ref[...]) + b_ref[...]
    o_ref[...] = jnp.minimum(f, 0.0) - jnp.log1p(jnp.exp(-jnp.abs(f)))


def _logf(a, w, b, geo):
    tm = geo.tile(1024)
    return pl.pallas_call(
        _logf_kernel,
        grid=(geo.m // tm,),
        in_specs=[pl.BlockSpec((tm, D), lambda i: (i, 0)), pl.BlockSpec((D, LANES), lambda i: (0, 0)),
                  pl.BlockSpec((1, LANES), lambda i: (0, 0))],
        out_specs=pl.BlockSpec((tm, LANES), lambda i: (i, 0)),
        out_shape=jax.ShapeDtypeStruct((geo.m, LANES), F32),
        compiler_params=_cp("arbitrary"),
        name="logf",
    )(a, w, b)


def _conv_in_kernel(a_ref, wb_ref, wc_ref, wu_ref, cw_ref, prev_ref, y_ref, st_ref, carry_ref, *, tpb):
    i, j = pl.program_id(0), pl.program_id(1)
    a = a_ref[...]
    bg = _dot(a, wb_ref[...])
    z = _dot(a, wc_ref[...]) * _dot(a, wu_ref[...])
    tm = z.shape[0]

    @pl.when(i % tpb == 0)
    def _():
        carry_ref[j] = prev_ref[0]

    prev = carry_ref[j]
    p0, p1 = prev[0:1], prev[1:2]
    row = lax.broadcasted_iota(jnp.int32, z.shape, 0)
    z1 = jnp.where(row == 0, p1, pltpu.roll(z, 1, 0))
    z2 = jnp.where(row == 0, p0, jnp.where(row == 1, p1, pltpu.roll(z, 2, 0)))
    cw = cw_ref[...]
    y = z2 * cw[0:1] + z1 * cw[1:2] + z * cw[2:3]
    y_ref[...] = (bg * y).astype(BF)
    last = z[tm - 2:tm]
    carry_ref[j] = last
    st_ref[0] = last


def _conv_in(h, w_in, conv_w, prev, geo):
    tm, tn = geo.tile(1024), 512
    nj = D // tn
    tpb = geo.tb // tm
    wspec = lambda off: pl.BlockSpec((D, tn), lambda i, j: (0, off * nj + j))
    y, tails = pl.pallas_call(
        functools.partial(_conv_in_kernel, tpb=tpb),
        grid=(geo.m // tm, nj),
        in_specs=[pl.BlockSpec((tm, D), lambda i, j: (i, 0)), wspec(0), wspec(1), wspec(2),
                  pl.BlockSpec((3, tn), lambda i, j: (0, j)),
                  pl.BlockSpec((1, 2, tn), lambda i, j: (i // tpb, 0, j))],
        out_specs=[pl.BlockSpec((tm, tn), lambda i, j: (i, j)),
                   pl.BlockSpec((1, 2, tn), lambda i, j: (i, 0, j))],
        out_shape=[jax.ShapeDtypeStruct((geo.m, D), BF),
                   jax.ShapeDtypeStruct((geo.m // tm, 2, D), F32)],
        scratch_shapes=[pltpu.VMEM((nj, 2, tn), F32)],
        compiler_params=_cp("arbitrary", "arbitrary"),
        name="conv_in",
    )(h, w_in, w_in, w_in, conv_w, prev)
    return y, tails[tpb - 1::tpb]


def _conv_in_step_kernel(a_ref, wb_ref, wc_ref, wu_ref, cw_ref, p0_ref, p1_ref, y_ref, z_ref):
    a = a_ref[...]
    bg = _dot(a, wb_ref[...])
    z = _dot(a, wc_ref[...]) * _dot(a, wu_ref[...])
    cw = cw_ref[...]
    y = p0_ref[...] * cw[0:1] + p1_ref[...] * cw[1:2] + z * cw[2:3]
    y_ref[...] = (bg * y).astype(BF)
    z_ref[...] = z


def _conv_in_step(h, w_in, conv_w, p0, p1, geo):
    tm, tn = geo.m, 512
    nj = D // tn
    wspec = lambda off: pl.BlockSpec((D, tn), lambda j: (0, off * nj + j))
    blk = pl.BlockSpec((tm, tn), lambda j: (0, j))
    return pl.pallas_call(
        _conv_in_step_kernel,
        grid=(nj,),
        in_specs=[pl.BlockSpec((tm, D), lambda j: (0, 0)), wspec(0), wspec(1), wspec(2),
                  pl.BlockSpec((3, tn), lambda j: (0, j)), blk, blk],
        out_specs=[blk, blk],
        out_shape=[jax.ShapeDtypeStruct((tm, D), BF), jax.ShapeDtypeStruct((tm, D), F32)],
        compiler_params=_cp("arbitrary"),
        name="conv_in_step",
    )(h, w_in, w_in, w_in, conv_w, p0, p1)


def _mlp_kernel(a_ref, wg_ref, wu_ref, wd_ref, *rest, gated):
    o_ref = rest[-1]
    f = pl.program_id(1)
    a = a_ref[...]
    g = _dot(a, wg_ref[...])
    u = _dot(a, wu_ref[...])
    hid = (g * (1.0 / (1.0 + jnp.exp(-g)))) * u
    if gated:
        gates = rest[0][...]
        lane = lax.broadcasted_iota(jnp.int32, gates.shape, 1)
        hid = hid * jnp.sum(jnp.where(lane == f, gates, 0.0), axis=-1, keepdims=True)
    part = _dot(hid.astype(BF), wd_ref[...])

    @pl.when(f == 0)
    def _():
        o_ref[...] = part

    @pl.when(f > 0)
    def _():
        o_ref[...] += part


def _mlp(h, wg, wu, wd, geo):
    tm, tf = geo.tile(1024), 512
    return pl.pallas_call(
        functools.partial(_mlp_kernel, gated=False),
        grid=(geo.m // tm, D_FF // tf),
        in_specs=[pl.BlockSpec((tm, D), lambda i, f: (i, 0)),
                  pl.BlockSpec((D, tf), lambda i, f: (0, f)),
                  pl.BlockSpec((D, tf), lambda i, f: (0, f)),
                  pl.BlockSpec((tf, D), lambda i, f: (f, 0))],
        out_specs=pl.BlockSpec((tm, D), lambda i, f: (i, 0)),
        out_shape=jax.ShapeDtypeStruct((geo.m, D), F32),
        compiler_params=_cp("arbitrary", "arbitrary"),
        name="mlp",
    )(h, wg, wu, wd)


def _moe(h, gates, wg, wu, wd, geo):
    tm = geo.tile(256)
    return pl.pallas_call(
        functools.partial(_mlp_kernel, gated=True),
        grid=(geo.m // tm, N_EXPERTS),
        in_specs=[pl.BlockSpec((tm, D), lambda i, e: (i, 0)),
                  pl.BlockSpec((None, D, EXPERT_FF), lambda i, e: (e, 0, 0)),
                  pl.BlockSpec((None, D, EXPERT_FF), lambda i, e: (e, 0, 0)),
                  pl.BlockSpec((None, EXPERT_FF, D), lambda i, e: (e, 0, 0)),
                  pl.BlockSpec((tm, LANES), lambda i, e: (i, 0))],
        out_specs=pl.BlockSpec((tm, D), lambda i, e: (i, 0)),
        out_shape=jax.ShapeDtypeStruct((geo.m, D), F32),
        compiler_params=_cp("arbitrary", "arbitrary"),
        name="moe",
    )(h, wg, wu, wd, gates)


MOE_TM = 256
RANK_TB = 512
DISPATCH_TB = 256
COMBINE_TM = 128


def _rank_kernel(ch_ref, rank_ref, cnt_ref, carry_ref):
    @pl.when(pl.program_id(0) == 0)
    def _():
        carry_ref[...] = jnp.zeros(carry_ref.shape, F32)

    ch = ch_ref[...]
    tb = ch.shape[0]
    below = lax.broadcasted_iota(jnp.int32, (tb, tb), 1) < lax.broadcasted_iota(jnp.int32, (tb, tb), 0)
    within = _dot(jnp.where(below, 1.0, 0.0).astype(BF), ch.astype(BF))
    rank_ref[...] = within + carry_ref[...]
    carry_ref[...] += jnp.sum(ch, axis=0, keepdims=True)
    cnt_ref[...] = carry_ref[...]


def _rank(chosen):
    m = chosen.shape[0]
    blk = pl.BlockSpec((RANK_TB, LANES), lambda i: (i, 0))
    return pl.pallas_call(
        _rank_kernel,
        grid=(m // RANK_TB,),
        in_specs=[blk],
        out_specs=[blk, pl.BlockSpec((1, LANES), lambda i: (0, 0))],
        out_shape=[jax.ShapeDtypeStruct((m, LANES), F32), jax.ShapeDtypeStruct((1, LANES), F32)],
        scratch_shapes=[pltpu.VMEM((1, LANES), F32)],
        compiler_params=_cp("arbitrary"),
        name="rank",
    )(chosen)


def _route_plan(gates, chosen, rank, cnt):
    m = gates.shape[0]
    nt = 2 * m // MOE_TM
    nv = nt + N_EXPERTS - 1
    i32 = jnp.int32
    e_ids = jnp.arange(N_EXPERTS, dtype=i32)
    ends = jnp.cumsum(cnt[0, :N_EXPERTS].astype(i32))
    off = jnp.concatenate([jnp.zeros((1,), i32), ends])
    pos = off[None, :N_EXPERTS] + rank[:, :N_EXPERTS].astype(i32)
    ch = chosen[:, :N_EXPERTS] > 0
    e_a = jnp.min(jnp.where(ch, e_ids, N_EXPERTS), axis=1, keepdims=True)
    e_b = jnp.max(jnp.where(ch, e_ids, -1), axis=1, keepdims=True)
    pick = lambda a, e: jnp.sum(jnp.where(e_ids == e, a, 0), axis=1)
    g8 = gates[:, :N_EXPERTS]
    gab = jnp.pad(jnp.stack([pick(g8, e_a), pick(g8, e_b)], axis=1), ((0, 0), (0, LANES - 2)))
    lo = jnp.arange(nt, dtype=i32) * MOE_TM
    e_lo = jnp.sum(ends[None, :] <= lo[:, None], axis=1).astype(i32)
    e_hi = jnp.sum(ends[None, :] <= (lo + MOE_TM - 1)[:, None], axis=1).astype(i32)
    v_end = jnp.cumsum(e_hi - e_lo + 1)
    v_start = v_end - (e_hi - e_lo + 1)
    v = jnp.arange(nv, dtype=i32)
    tile = jnp.minimum(jnp.sum(v_end[None, :] <= v[:, None], axis=1), nt - 1).astype(i32)
    expert = jnp.minimum(e_lo[tile] + v - v_start[tile], N_EXPERTS - 1).astype(i32)
    valid = v < v_end[-1]
    first = jnp.concatenate([jnp.ones((1,), bool), tile[1:] != tile[:-1]]) & valid
    return dict(pos_a=pick(pos, e_a).astype(i32), pos_b=pick(pos, e_b).astype(i32), gab=gab, off=off,
                tile=tile, expert=expert, valid=valid.astype(i32), first=first.astype(i32))


def _dispatch_kernel(pa_ref, pb_ref, h_ref, xs_ref, sems):
    base = pl.program_id(0) * DISPATCH_TB

    def start(r, c):
        src = h_ref.at[pl.ds(base + r, 1), :]
        pltpu.make_async_copy(src, xs_ref.at[pl.ds(pa_ref[base + r], 1), :], sems.at[0]).start()
        pltpu.make_async_copy(src, xs_ref.at[pl.ds(pb_ref[base + r], 1), :], sems.at[0]).start()
        return c

    lax.fori_loop(0, DISPATCH_TB, start, 0)
    n = 2 * DISPATCH_TB
    pltpu.make_async_copy(h_ref.at[pl.ds(0, n), :], xs_ref.at[pl.ds(0, n), :], sems.at[0]).wait()


def _dispatch(plan, h):
    m = h.shape[0]
    any_spec = pl.BlockSpec(memory_space=pl.ANY)
    return pl.pallas_call(
        _dispatch_kernel,
        grid_spec=pltpu.PrefetchScalarGridSpec(
            num_scalar_prefetch=2, grid=(m // DISPATCH_TB,), in_specs=[any_spec], out_specs=any_spec,
            scratch_shapes=[pltpu.SemaphoreType.DMA((1,))]),
        out_shape=jax.ShapeDtypeStruct((2 * m, D), F32),
        compiler_params=_cp("arbitrary"),
        name="dispatch",
    )(plan["pos_a"], plan["pos_b"], h)


def _expert_mm_kernel(tile_ref, exp_ref, first_ref, valid_ref, off_ref, x_ref, wg_ref, wu_ref, wd_ref, o_ref):
    v = pl.program_id(0)

    @pl.when(valid_ref[v] == 1)
    def _():
        e = exp_ref[v]
        rows = tile_ref[v] * MOE_TM + lax.broadcasted_iota(jnp.int32, (MOE_TM, 1), 0)
        mine = (rows >= off_ref[e]) & (rows < off_ref[e + 1])
        x = x_ref[...].astype(BF)
        g = _dot(x, wg_ref[...])
        u = _dot(x, wu_ref[...])
        hid = jnp.where(mine, (g * (1.0 / (1.0 + jnp.exp(-g)))) * u, 0.0)
        part = _dot(hid.astype(BF), wd_ref[...])

        @pl.when(first_ref[v] == 1)
        def _():
            o_ref[...] = part

        @pl.when(first_ref[v] == 0)
        def _():
            o_ref[...] += part


def _expert_mm(plan, xs, wg, wu, wd):
    nv = plan["tile"].shape[0]
    row = pl.BlockSpec((MOE_TM, D), lambda v, t, e, f, va, o: (t[v], 0))
    w_in = pl.BlockSpec((None, D, EXPERT_FF), lambda v, t, e, f, va, o: (e[v], 0, 0))
    w_out = pl.BlockSpec((None, EXPERT_FF, D), lambda v, t, e, f, va, o: (e[v], 0, 0))
    return pl.pallas_call(
        _expert_mm_kernel,
        grid_spec=pltpu.PrefetchScalarGridSpec(
            num_scalar_prefetch=5, grid=(nv,), in_specs=[row, w_in, w_in, w_out], out_specs=row),
        out_shape=jax.ShapeDtypeStruct(xs.shape, F32),
        compiler_params=_cp("arbitrary"),
        name="expert_mm",
    )(plan["tile"], plan["expert"], plan["first"], plan["valid"], plan["off"], xs, wg, wu, wd)


def _combine_kernel(pa_ref, pb_ref, x_ref, gab_ref, gpost_ref, gate_ref, z_ref, o_ref, za_ref, zb_ref, sems):
    i, n = pl.program_id(0), pl.num_programs(0)
    tm = COMBINE_TM

    def start_all(step, slot):
        def body(r, c):
            t = step * tm + r
            pltpu.make_async_copy(z_ref.at[pl.ds(pa_ref[t], 1), :], za_ref.at[slot, pl.ds(r, 1), :],
                                  sems.at[slot]).start()
            pltpu.make_async_copy(z_ref.at[pl.ds(pb_ref[t], 1), :], zb_ref.at[slot, pl.ds(r, 1), :],
                                  sems.at[slot]).start()
            return c
        lax.fori_loop(0, tm, body, 0)

    @pl.when(i == 0)
    def _():
        start_all(0, 0)

    @pl.when(i + 1 < n)
    def _():
        start_all(i + 1, (i + 1) % 2)

    slot = i % 2
    for buf in (za_ref, zb_ref):
        pltpu.make_async_copy(z_ref.at[pl.ds(0, tm), :], buf.at[slot], sems.at[slot]).wait()
    gab = gab_ref[...]
    f = gab[:, 0:1] * za_ref[slot] + gab[:, 1:2] * zb_ref[slot]
    rf = lax.rsqrt(jnp.mean(f * f, axis=-1, keepdims=True) + EPS)
    o_ref[...] = x_ref[...] + gate_ref[0] * (f * rf * gpost_ref[...])


def _combine_epi(plan, x, z, g_post, gate, geo):
    mod_gate, gate_k = gate
    tm = COMBINE_TM
    tpb = geo.tb // tm
    row = pl.BlockSpec((tm, D), lambda i, pa, pb: (i, 0))
    return pl.pallas_call(
        _combine_kernel,
        grid_spec=pltpu.PrefetchScalarGridSpec(
            num_scalar_prefetch=2, grid=(geo.m // tm,),
            in_specs=[row, pl.BlockSpec((tm, LANES), lambda i, pa, pb: (i, 0)),
                      pl.BlockSpec((1, D), lambda i, pa, pb: (0, 0)),
                      pl.BlockSpec((1, geo.r, D), lambda i, pa, pb: (i // tpb, 0, gate_k)),
                      pl.BlockSpec(memory_space=pl.ANY)],
            out_specs=row,
            scratch_shapes=[pltpu.VMEM((2, tm, D), F32), pltpu.VMEM((2, tm, D), F32),
                            pltpu.SemaphoreType.DMA((2,))]),
        out_shape=jax.ShapeDtypeStruct((geo.m, D), F32),
        compiler_params=_cp("arbitrary"),
        name="combine",
    )(plan["pos_a"], plan["pos_b"], x, plan["gab"], g_post.reshape(1, D), mod_gate, z)


def _cumsum_lanes(x):
    n = x.shape[-1]
    lane = lax.broadcasted_iota(jnp.int32, x.shape, x.ndim - 1)
    s = 1
    while s < n:
        x = x + jnp.where(lane >= s, pltpu.roll(x, s, x.ndim - 1), 0.0)
        s *= 2
    return x


def _cumsum_kernel(x_ref, o_ref, *, scale):
    o_ref[0] = _cumsum_lanes(x_ref[0]) * scale


def _cumsum(lf_t, scale):
    b, h, t = lf_t.shape
    spec = pl.BlockSpec((1, h, t), lambda i: (i, 0, 0))
    return pl.pallas_call(
        functools.partial(_cumsum_kernel, scale=scale), grid=(b,), in_specs=[spec], out_specs=spec,
        out_shape=jax.ShapeDtypeStruct(lf_t.shape, F32),
        compiler_params=_cp("arbitrary"), name="cumsum",
    )(lf_t)


def _flash_kernel(q_ref, k_ref, v_ref, fk_ref, o_ref, kb_ref, vb_ref, *, tq, tk):
    qi = pl.program_id(2)

    @pl.when(qi == 0)
    def _():
        kb_ref[...] = k_ref[...].astype(BF)
        vb_ref[...] = v_ref[...].astype(BF)

    q = q_ref[...]

    def step(kj, carry, masked):
        m, l, acc = carry
        start = pl.multiple_of(kj * tk, tk)
        k = kb_ref[pl.ds(start, tk), :]
        v = vb_ref[pl.ds(start, tk), :]
        s = lax.dot_general(q, k, NT_DIMS, preferred_element_type=F32) - fk_ref[0, kj]
        if masked:
            row = lax.broadcasted_iota(jnp.int32, s.shape, 0) + qi * tq
            col = lax.broadcasted_iota(jnp.int32, s.shape, 1) + kj * tk
            s = jnp.where(col <= row, s, -jnp.inf)
        m_new = jnp.maximum(m, jnp.max(s, axis=-1, keepdims=True))
        alpha = jnp.exp2(m - m_new)
        p = jnp.exp2(s - m_new)
        l = alpha * l + jnp.sum(p, axis=-1, keepdims=True)
        acc = alpha * acc + _dot(p.astype(BF), v)
        return m_new, l, acc

    init = (jnp.full((tq, 1), -jnp.inf, F32), jnp.zeros((tq, 1), F32), jnp.zeros((tq, HEAD_DIM), F32))
    n_full = (qi * tq) // tk
    carry = lax.fori_loop(0, n_full, lambda kj, c: step(kj, c, False), init)
    _, l, acc = step(n_full, carry, True)
    o_ref[...] = (acc / l).astype(o_ref.dtype)


FLASH_TQ = 512
FLASH_TK = 512


def _flash(q, k, v, fk, nb, t):
    tq, tk = FLASH_TQ, FLASH_TK
    assert tk % tq == 0
    nq = t // tq
    return pl.pallas_call(
        functools.partial(_flash_kernel, tq=tq, tk=tk),
        grid=(nb, N_HEADS, nq),
        in_specs=[pl.BlockSpec((tq, HEAD_DIM), lambda b, h, i: (b * nq + i, h)),
                  pl.BlockSpec((t, HEAD_DIM), lambda b, h, i: (b, h)),
                  pl.BlockSpec((t, HEAD_DIM), lambda b, h, i: (b, h)),
                  pl.BlockSpec((1, t // tk, 1, tk), lambda b, h, i: (b * N_HEADS + h, 0, 0, 0))],
        out_specs=pl.BlockSpec((tq, HEAD_DIM), lambda b, h, i: (b * nq + i, h)),
        out_shape=jax.ShapeDtypeStruct((nb * t, D), BF),
        scratch_shapes=[pltpu.VMEM((t, HEAD_DIM), BF), pltpu.VMEM((t, HEAD_DIM), BF)],
        compiler_params=_cp("arbitrary", "arbitrary", "arbitrary"),
        name="flash",
    )(q, k, v, fk)


PAGE_COLS = PAGE * N_HEADS
PAGES_PER_STEP = 4


def _head_cumsum(x):
    lane = lax.broadcasted_iota(jnp.int32, x.shape, 1)
    s = N_HEADS
    while s < PAGE_COLS:
        x = x + jnp.where(lane >= s, pltpu.roll(x, s, 1), 0.0)
        s *= 2
    return x


def _head_totals(x):
    lane = lax.broadcasted_iota(jnp.int32, x.shape, 1)
    y = jnp.where(lane >= PAGE_COLS - N_HEADS, x, 0.0)
    s = N_HEADS
    while s < PAGE_COLS:
        y = y + pltpu.roll(y, PAGE_COLS - s, 1)
        s *= 2
    return y


def _decode_kernel(pt_ref, q_ref, kn_ref, vn_ref, lfn_ref, *refs):
    n = PAGES_PER_STEP
    k_refs, v_refs, lf_refs = refs[:n], refs[n:2 * n], refs[2 * n:3 * n]
    o_ref, qb_ref, m_ref, l_ref, acc_ref, fc_ref, kb_ref, vb_ref = refs[3 * n:]
    p = pl.program_id(1)
    lane = lax.broadcasted_iota(jnp.int32, (N_HEADS, n * PAGE_COLS), 1)
    head = lax.broadcasted_iota(jnp.int32, (N_HEADS, n * PAGE_COLS), 0)
    own = lane % N_HEADS == head

    @pl.when(p == 0)
    def _():
        qb_ref[...] = q_ref[0].astype(BF)
        m_ref[...] = jnp.full(m_ref.shape, -jnp.inf, F32)
        l_ref[...] = jnp.zeros(l_ref.shape, F32)
        acc_ref[...] = jnp.zeros(acc_ref.shape, F32)
        fc_ref[...] = jnp.zeros(fc_ref.shape, F32)

    qb = qb_ref[...]
    for i in range(n):
        kb_ref[i * PAGE_COLS:(i + 1) * PAGE_COLS, :] = k_refs[i][0].astype(BF)
        vb_ref[i * PAGE_COLS:(i + 1) * PAGE_COLS, :] = v_refs[i][0].astype(BF)
    s = lax.dot_general(qb, kb_ref[...], NT_DIMS, preferred_element_type=F32)
    cs = _head_cumsum(jnp.concatenate([r[0] for r in lf_refs], axis=0))
    tot = _head_totals(cs)
    run = fc_ref[...]
    fps = []
    for i in range(n):
        fps.append(cs[i:i + 1] + run)
        run = run + tot[i:i + 1]
    fc = run
    s = jnp.where(own, s * ATTN_SCALE - jnp.concatenate(fps, axis=1), -jnp.inf)
    m_old = m_ref[...]
    m = jnp.maximum(m_old, jnp.max(s, axis=-1, keepdims=True))
    alpha = jnp.exp(m_old - m)
    pr = jnp.exp(s - m[:, 0:1])
    l = alpha * l_ref[...] + jnp.sum(pr, axis=-1, keepdims=True)
    acc = alpha * acc_ref[...] + _dot(pr.astype(BF), vb_ref[...])
    m_ref[...] = m
    l_ref[...] = l
    acc_ref[...] = acc
    fc_ref[...] = fc

    @pl.when(p == pl.num_programs(1) - 1)
    def _():
        kn = kn_ref[0].astype(BF).astype(F32)
        vn = vn_ref[0].astype(BF).astype(F32)
        on_head = (lax.broadcasted_iota(jnp.int32, (N_HEADS, PAGE_COLS), 1)
                   == lax.broadcasted_iota(jnp.int32, (N_HEADS, PAGE_COLS), 0))
        f_past = jnp.sum(jnp.where(on_head, fc, 0.0), axis=-1, keepdims=True)
        s_n = jnp.sum(qb.astype(F32) * kn, axis=-1, keepdims=True) * ATTN_SCALE
        s_n = s_n - (f_past + lfn_ref[0])
        m1 = m[:, 0:1]
        m2 = jnp.maximum(m1, s_n)
        a2 = jnp.exp(m1 - m2)
        pn = jnp.exp(s_n - m2)
        l2 = a2 * l[:, 0:1] + pn
        acc2 = a2 * acc + pn.astype(BF).astype(F32) * vn
        o_ref[0] = (acc2 / l2).astype(o_ref.dtype)


def _decode(page_table, q, kn, vn, lfn, cache_k, cache_v, cache_lf):
    nb, n_pages = page_table.shape
    n = PAGES_PER_STEP
    tok = pl.BlockSpec((1, N_HEADS, HEAD_DIM), lambda b, p, pt: (b, 0, 0))
    page = lambda i: pl.BlockSpec((1, PAGE_COLS, HEAD_DIM), lambda b, p, pt: (pt[b, n * p + i], 0, 0))
    lfpage = lambda i: pl.BlockSpec((1, 1, PAGE_COLS), lambda b, p, pt: (pt[b, n * p + i], 0, 0))
    grid_spec = pltpu.PrefetchScalarGridSpec(
        num_scalar_prefetch=1,
        grid=(nb, n_pages // n),
        in_specs=[tok, tok, tok, pl.BlockSpec((1, N_HEADS, 1), lambda b, p, pt: (b, 0, 0))]
        + [page(i) for i in range(n)] + [page(i) for i in range(n)] + [lfpage(i) for i in range(n)],
        out_specs=tok,
        scratch_shapes=[pltpu.VMEM((N_HEADS, HEAD_DIM), BF),
                        pltpu.VMEM((N_HEADS, LANES), F32),
                        pltpu.VMEM((N_HEADS, LANES), F32),
                        pltpu.VMEM((N_HEADS, HEAD_DIM), F32),
                        pltpu.VMEM((1, PAGE_COLS), F32),
                        pltpu.VMEM((n * PAGE_COLS, HEAD_DIM), BF),
                        pltpu.VMEM((n * PAGE_COLS, HEAD_DIM), BF)],
    )
    return pl.pallas_call(
        _decode_kernel,
        grid_spec=grid_spec,
        out_shape=jax.ShapeDtypeStruct((nb, N_HEADS, HEAD_DIM), BF),
        compiler_params=_cp("arbitrary", "arbitrary"),
        name="decode",
    )(page_table, q, kn, vn, lfn, *([cache_k] * n), *([cache_v] * n), *([cache_lf] * n))


def _moe_dense(x, h, gates, chosen, g_post, gate, w, geo):
    del chosen
    f = _moe(h, gates, w["wg_e"], w["wu_e"], w["wd_e"], geo)
    (x,) = _epi(x, f, g_post, gate, [], geo)
    return x


def _moe_top2(x, h, gates, chosen, g_post, gate, w, geo):
    rank, cnt = _rank(chosen)
    plan = _route_plan(gates, chosen, rank, cnt)
    z = _expert_mm(plan, _dispatch(plan, h), w["wg_e"], w["wu_e"], w["wd_e"])
    return _combine_epi(plan, x, z, g_post, gate, geo)


_moe_dense.h_dtype = BF
_moe_top2.h_dtype = F32


def _trunk(x, geo, mod0, mod1, modkv, w, conv_fn, attend_fn, moe_fn):
    g_norm = w["g_norm"]
    h = _norm_mod(x, g_norm[0, 0], mod0, 1, 0, geo)
    y, conv_state = conv_fn(h)
    o = _mm(y, w["w_out"], F32, geo)
    x, h = _epi(x, o, g_norm[0, 1], (mod0, 2), [(g_norm[0, 2], mod0, 4, 3)], geo)
    f = _mlp(h, w["wg_d"], w["wu_d"], w["wd_d"], geo)
    x, hq, hkv = _epi(x, f, g_norm[0, 3], (mod0, 5),
                      [(g_norm[1, 0], mod1, 1, 0), (w["g_kv"], modkv, 1, 0)], geo)
    k = _mm(hkv, w["wk"], F32, geo)
    v = _mm(hkv, w["wv"], F32, geo)
    lf = _logf(hkv, w["wf"], w["bf"], geo)[:, :N_HEADS]
    q = _mm(hq, w["wq"], attend_fn.q_dtype, geo, scale=attend_fn.q_scale)
    att = attend_fn(q, k, v, lf)
    o = _mm(att, w["wo"], F32, geo)
    x, h, gates, chosen = _epi(x, o, g_norm[1, 1], (mod1, 2), [(g_norm[1, 2], mod1, 4, 3)], geo,
                               router=(w["wr"], w["br"]), h_dtype=moe_fn.h_dtype)
    x = moe_fn(x, h, gates, chosen, g_norm[1, 3], (mod1, 5), w, geo)
    return x, conv_state, k, v, lf


def kernel(x_prompt, x_sample, state_conv, cache_k, cache_v, cache_logf, page_table, c_prompt, c_sample,
           w_mod, b_mod, g_norm, w_in_a, conv_w, w_out_a, g_kv, w_mod_kv, b_mod_kv, w_kv, b_f,
           w_q, w_o, w_gate_d, w_up_d, w_down_d, w_router, b_router, w_gate_e, w_up_e, w_down_e):
    nb_p, t_p, _ = x_prompt.shape
    nb_s = x_sample.shape[0]
    pad_s = SAMPLE_ROWS - nb_s
    n_phys = cache_k.shape[0]

    w = {
        "g_norm": g_norm, "g_kv": g_kv,
        "w_out": w_out_a[0].astype(BF),
        "wk": w_kv[:, :D].astype(BF), "wv": w_kv[:, D:2 * D].astype(BF),
        "wf": jnp.pad(w_kv[:, 2 * D:], ((0, 0), (0, LANES - N_HEADS))).astype(BF),
        "bf": jnp.pad(b_f, (0, LANES - N_HEADS)).reshape(1, LANES),
        "wq": w_q[0].astype(BF), "wo": w_o[0].astype(BF),
        "wg_d": w_gate_d[0].astype(BF), "wu_d": w_up_d[0].astype(BF), "wd_d": w_down_d[0].astype(BF),
        "wr": jnp.pad(w_router[0], ((0, 0), (0, LANES - N_EXPERTS))),
        "br": jnp.pad(b_router[0], (0, LANES - N_EXPERTS)).reshape(1, LANES),
        "wg_e": w_gate_e[0].astype(BF), "wu_e": w_up_e[0].astype(BF), "wd_e": w_down_e[0].astype(BF),
    }
    w_in = w_in_a[0].astype(BF)
    cw = conv_w[0]

    c16 = jnp.concatenate([c_prompt, c_sample, jnp.zeros((SAMPLE_ROWS - nb_p - nb_s, D), F32)], axis=0)
    mod = _modvec(c16, w_mod, b_mod)
    modkv = _modvec(c16, w_mod_kv[None], b_mod_kv[None])[0]

    def split(m):
        m_p = m[:nb_p][:, None, :]
        m_s = jnp.pad(m[nb_p:nb_p + nb_s], ((0, pad_s), (0, 0)))[None]
        return m_p, m_s

    mod0_p, mod0_s = split(mod[0])
    mod1_p, mod1_s = split(mod[1])
    modkv_p, modkv_s = split(modkv)

    geo_p = Geo(nb_p * t_p, t_p, 1)
    zeros_prev = jnp.zeros((nb_p, 2, D), F32)

    def attend_p(q, k, v, lf):
        lf_t = jnp.transpose(lf.reshape(nb_p, t_p, N_HEADS), (0, 2, 1))
        fk = _cumsum(lf_t, LOG2E).reshape(nb_p * N_HEADS, t_p // FLASH_TK, 1, FLASH_TK)
        return _flash(q, k, v, fk, nb_p, t_p)

    attend_p.q_dtype, attend_p.q_scale = BF, ATTN_SCALE * LOG2E
    y_p, conv_p, k_p, v_p, lf_p = _trunk(
        x_prompt.reshape(nb_p * t_p, D), geo_p, mod0_p, mod1_p, modkv_p, w,
        lambda h: _conv_in(h, w_in, cw, zeros_prev, geo_p), attend_p, _moe_top2)

    geo_s = Geo(SAMPLE_ROWS, SAMPLE_ROWS, SAMPLE_ROWS)
    pad_rows = lambda a: jnp.pad(a, ((0, pad_s), (0, 0)))
    prev0 = pad_rows(state_conv[0, :, 0])
    prev1 = pad_rows(state_conv[0, :, 1])
    ck = cache_k.reshape(n_phys, PAGE_COLS, HEAD_DIM)
    cv = cache_v.reshape(n_phys, PAGE_COLS, HEAD_DIM)
    clf = cache_logf.reshape(n_phys, 1, PAGE_COLS)
    heads = lambda a: a[:nb_s].reshape(nb_s, N_HEADS, HEAD_DIM)

    def attend_s(q, k, v, lf):
        o = _decode(page_table, heads(q), heads(k), heads(v), lf[:nb_s, :, None], ck, cv, clf)
        return pad_rows(o.reshape(nb_s, D))

    attend_s.q_dtype, attend_s.q_scale = F32, 1.0
    y_s, z_s, k_s, v_s, lf_s = _trunk(
        pad_rows(x_sample[:, 0, :]), geo_s, mod0_s, mod1_s, modkv_s, w,
        lambda h: _conv_in_step(h, w_in, cw, prev0, prev1, geo_s), attend_s, _moe_dense)

    conv_s = jnp.stack([state_conv[0, :, 1], z_s[:nb_s]], axis=1)[None]
    return (y_p.reshape(nb_p, t_p, D),
            y_s[:nb_s].reshape(nb_s, 1, D),
            conv_p[None],
            conv_s,
            k_p.reshape(nb_p, t_p, N_HEADS, HEAD_DIM),
            v_p.reshape(nb_p, t_p, N_HEADS, HEAD_DIM),
            lf_p.reshape(nb_p, t_p, N_HEADS),
            k_s[:nb_s].reshape(nb_s, 1, N_HEADS, HEAD_DIM),
            v_s[:nb_s].reshape(nb_s, 1, N_HEADS, HEAD_DIM),
            lf_s[:nb_s].reshape(nb_s, 1, N_HEADS))
```

```python
import functools

import jax
import jax.numpy as jnp
from jax import lax
from jax.experimental import pallas as pl
from jax.experimental.pallas import tpu as pltpu

D = 2048
N_HEADS = 16
HEAD_DIM = 128
D_FF = 5632
N_EXPERTS = 8
EXPERT_FF = 1408
PAGE = 128
EPS = 1e-6
ATTN_SCALE = HEAD_DIM ** -0.5
LOG2E = 1.4426950408889634
LANES = 128
SAMPLE_ROWS = 16
VMEM_LIMIT = 56 * 1024 * 1024

BF = jnp.bfloat16
F32 = jnp.float32
NT_DIMS = (((1,), (1,)), ((), ()))


def _cp(*sem):
    return pltpu.CompilerParams(dimension_semantics=sem, vmem_limit_bytes=VMEM_LIMIT)


def _dot(a, b):
    return jnp.dot(a, b, preferred_element_type=F32)


class Geo:
    def __init__(self, m, rows_per_batch, mod_rows):
        self.m = m
        self.tb = rows_per_batch
        self.r = mod_rows

    def tile(self, pref):
        return min(pref, self.m)

    def mod_spec(self, tm, k):
        tpb = self.tb // tm
        return pl.BlockSpec((1, self.r, D), lambda i, *_: (i // tpb, 0, k))


def _vec_spec():
    return pl.BlockSpec((1, D), lambda i, *_: (0, 0))


def _modvec_kernel(c_ref, w_ref, b_ref, o_ref):
    o_ref[...] = _dot(c_ref[...].astype(BF), w_ref[...].astype(BF)) + b_ref[...]


def _modvec(c16, w, b):
    nl, _, n = w.shape
    tn = 1024
    return pl.pallas_call(
        _modvec_kernel,
        grid=(nl, n // tn),
        in_specs=[pl.BlockSpec((SAMPLE_ROWS, D), lambda l, j: (0, 0)),
                  pl.BlockSpec((None, D, tn), lambda l, j: (l, 0, j)),
                  pl.BlockSpec((None, 1, tn), lambda l, j: (l, 0, j))],
        out_specs=pl.BlockSpec((None, SAMPLE_ROWS, tn), lambda l, j: (l, 0, j)),
        out_shape=jax.ShapeDtypeStruct((nl, SAMPLE_ROWS, n), F32),
        compiler_params=_cp("arbitrary", "arbitrary"),
        name="modvec",
    )(c16, w, b.reshape(nl, 1, n))


def _norm_mod_kernel(x_ref, g_ref, sc_ref, sh_ref, h_ref):
    x = x_ref[...]
    r = lax.rsqrt(jnp.mean(x * x, axis=-1, keepdims=True) + EPS)
    h = (x * r * g_ref[...]) * (1.0 + sc_ref[0]) + sh_ref[0]
    h_ref[...] = h.astype(BF)


def _norm_mod(x, g, mod, sc_k, sh_k, geo):
    tm = geo.tile(512)
    row = pl.BlockSpec((tm, D), lambda i: (i, 0))
    return pl.pallas_call(
        _norm_mod_kernel,
        grid=(geo.m // tm,),
        in_specs=[row, _vec_spec(), geo.mod_spec(tm, sc_k), geo.mod_spec(tm, sh_k)],
        out_specs=row,
        out_shape=jax.ShapeDtypeStruct((geo.m, D), BF),
        compiler_params=_cp("arbitrary"),
        name="norm_mod",
    )(x, g.reshape(1, D), mod, mod)


def _top2_gates(h, wr, br):
    logits = jnp.dot(h, wr, preferred_element_type=F32, precision=lax.Precision.HIGHEST) + br
    lane = lax.broadcasted_iota(jnp.int32, logits.shape, 1)
    valid = lane < N_EXPERTS
    lg = jnp.where(valid, logits, -jnp.inf)
    e = jnp.exp(lg - jnp.max(lg, axis=-1, keepdims=True))
    p = e / jnp.sum(e, axis=-1, keepdims=True)
    p = jnp.where(valid, p, -1.0)
    v1 = jnp.max(p, axis=-1, keepdims=True)
    i1 = jnp.min(jnp.where(p == v1, lane, LANES), axis=-1, keepdims=True)
    p2 = jnp.where(lane == i1, -1.0, p)
    v2 = jnp.max(p2, axis=-1, keepdims=True)
    i2 = jnp.min(jnp.where(p2 == v2, lane, LANES), axis=-1, keepdims=True)
    den = v1 + v2
    gates = jnp.where(lane == i1, v1 / den, 0.0) + jnp.where(lane == i2, v2 / den, 0.0)
    chosen = jnp.where((lane == i1) | (lane == i2), 1.0, 0.0)
    return gates, chosen


def _epi_kernel(*refs, n_pro, router, fused):
    if fused:
        x_ref, a_ref, w_ref, gpost_ref, gate_ref = refs[:5]
        f = _dot(a_ref[...], w_ref[...])
    else:
        x_ref, f_ref, gpost_ref, gate_ref = refs[:4]
        f = f_ref[...]
    pos = 5 if fused else 4
    pro = [refs[pos + 3 * k: pos + 3 * k + 3] for k in range(n_pro)]
    pos += 3 * n_pro
    if router:
        wr_ref, br_ref = refs[pos:pos + 2]
        pos += 2
    xo_ref = refs[pos]
    h_refs = refs[pos + 1: pos + 1 + n_pro]
    rf = lax.rsqrt(jnp.mean(f * f, axis=-1, keepdims=True) + EPS)
    xn = x_ref[...] + gate_ref[0] * (f * rf * gpost_ref[...])
    xo_ref[...] = xn
    if n_pro:
        xr = xn * lax.rsqrt(jnp.mean(xn * xn, axis=-1, keepdims=True) + EPS)
        for k, (g_ref, sc_ref, sh_ref) in enumerate(pro):
            h = (xr * g_ref[...]) * (1.0 + sc_ref[0]) + sh_ref[0]
            h_refs[k][...] = h.astype(h_refs[k].dtype)
            if router and k == 0:
                gates, chosen = _top2_gates(h, wr_ref[...], br_ref[...])
                refs[pos + 1 + n_pro][...] = gates
                refs[pos + 2 + n_pro][...] = chosen


def _epi(x, f, g_post, gate, pro, geo, router=None, h_dtype=BF):
    mod_gate, gate_k = gate
    tm = geo.tile(256)
    row = pl.BlockSpec((tm, D), lambda i: (i, 0))
    fused = isinstance(f, tuple)
    if fused:
        in_specs = [row, row, pl.BlockSpec((D, D), lambda i: (0, 0)), _vec_spec(), geo.mod_spec(tm, gate_k)]
        args = [x, f[0], f[1], g_post.reshape(1, D), mod_gate]
    else:
        in_specs = [row, row, _vec_spec(), geo.mod_spec(tm, gate_k)]
        args = [x, f, g_post.reshape(1, D), mod_gate]
    for g, mod, sc_k, sh_k in pro:
        in_specs += [_vec_spec(), geo.mod_spec(tm, sc_k), geo.mod_spec(tm, sh_k)]
        args += [g.reshape(1, D), mod, mod]
    out_specs = [row] + [row] * len(pro)
    out_shape = [jax.ShapeDtypeStruct((geo.m, D), F32)] + [jax.ShapeDtypeStruct((geo.m, D), h_dtype)] * len(pro)
    if router is not None:
        wr, br = router
        in_specs += [pl.BlockSpec((D, LANES), lambda i: (0, 0)), pl.BlockSpec((1, LANES), lambda i: (0, 0))]
        args += [wr, br]
        out_specs += [pl.BlockSpec((tm, LANES), lambda i: (i, 0))] * 2
        out_shape += [jax.ShapeDtypeStruct((geo.m, LANES), F32)] * 2
    return pl.pallas_call(
        functools.partial(_epi_kernel, n_pro=len(pro), router=router is not None, fused=fused),
        grid=(geo.m // tm,),
        in_specs=in_specs,
        out_specs=out_specs,
        out_shape=out_shape,
        compiler_params=_cp("arbitrary"),
        name="epi",
    )(*args)


def _mm_kernel(a_ref, w_ref, o_ref, wb_ref, *, scale):
    @pl.when(pl.program_id(1) == 0)
    def _():
        wb_ref[...] = w_ref[...].astype(BF)

    acc = _dot(a_ref[...], wb_ref[...])
    if scale != 1.0:
        acc = acc * scale
    o_ref[...] = acc.astype(o_ref.dtype)


def _mm(a, w, out_dtype, geo, scale=1.0, col0=0):
    tm, tn = geo.tile(1024), 1024
    jb = col0 // tn
    return pl.pallas_call(
        functools.partial(_mm_kernel, scale=scale),
        grid=(D // tn, geo.m // tm),
        in_specs=[pl.BlockSpec((tm, D), lambda j, i: (i, 0)), pl.BlockSpec((D, tn), lambda j, i: (0, jb + j))],
        out_specs=pl.BlockSpec((tm, tn), lambda j, i: (i, j)),
        out_shape=jax.ShapeDtypeStruct((geo.m, D), out_dtype),
        scratch_shapes=[pltpu.VMEM((D, tn), BF)],
        compiler_params=_cp("arbitrary", "arbitrary"),
        name="mm",
    )(a, w)


def _logf_kernel(a_ref, w_ref, b_ref, o_ref):
    f = _dot(a_ref[...], w_ref[...]) + b_ref[...]
    o_ref[...] = jnp.minimum(f, 0.0) - jnp.log1p(jnp.exp(-jnp.abs(f)))


def _logf(a, w, b, geo):
    tm = geo.tile(1024)
    return pl.pallas_call(
        _logf_kernel,
        grid=(geo.m // tm,),
        in_specs=[pl.BlockSpec((tm, D), lambda i: (i, 0)), pl.BlockSpec((D, LANES), lambda i: (0, 0)),
                  pl.BlockSpec((1, LANES), lambda i: (0, 0))],
        out_specs=pl.BlockSpec((tm, LANES), lambda i: (i, 0)),
        out_shape=jax.ShapeDtypeStruct((geo.m, LANES), F32),
        compiler_params=_cp("arbitrary"),
        name="logf",
    )(a, w, b)


def _conv_in_kernel(a_ref, wb_ref, wc_ref, wu_ref, cw_ref, prev_ref, y_ref, st_ref, w3_ref, carry_ref, *, tpb):
    i = pl.program_id(1)

    @pl.when(i == 0)
    def _():
        for k, w_ref in enumerate((wb_ref, wc_ref, wu_ref)):
            w3_ref[k] = w_ref[...].astype(BF)

    a = a_ref[...]
    bg = _dot(a, w3_ref[0])
    z = _dot(a, w3_ref[1]) * _dot(a, w3_ref[2])
    tm = z.shape[0]

    @pl.when(i % tpb == 0)
    def _():
        carry_ref[...] = prev_ref[0]

    prev = carry_ref[...]
    p0, p1 = prev[0:1], prev[1:2]
    row = lax.broadcasted_iota(jnp.int32, z.shape, 0)
    z1 = jnp.where(row == 0, p1, pltpu.roll(z, 1, 0))
    z2 = jnp.where(row == 0, p0, jnp.where(row == 1, p1, pltpu.roll(z, 2, 0)))
    cw = cw_ref[...]
    y = z2 * cw[0:1] + z1 * cw[1:2] + z * cw[2:3]
    y_ref[...] = (bg * y).astype(BF)
    last = z[tm - 2:tm]
    carry_ref[...] = last
    st_ref[0] = last


def _conv_in(h, w_in, conv_w, prev, geo):
    tm, tn = geo.tile(512), 512
    nj = D // tn
    tpb = geo.tb // tm
    wspec = lambda off: pl.BlockSpec((D, tn), lambda j, i: (0, off * nj + j))
    y, tails = pl.pallas_call(
        functools.partial(_conv_in_kernel, tpb=tpb),
        grid=(nj, geo.m // tm),
        in_specs=[pl.BlockSpec((tm, D), lambda j, i: (i, 0)), wspec(0), wspec(1), wspec(2),
                  pl.BlockSpec((3, tn), lambda j, i: (0, j)),
                  pl.BlockSpec((1, 2, tn), lambda j, i: (i // tpb, 0, j))],
        out_specs=[pl.BlockSpec((tm, tn), lambda j, i: (i, j)),
                   pl.BlockSpec((1, 2, tn), lambda j, i: (i, 0, j))],
        out_shape=[jax.ShapeDtypeStruct((geo.m, D), BF),
                   jax.ShapeDtypeStruct((geo.m // tm, 2, D), F32)],
        scratch_shapes=[pltpu.VMEM((3, D, tn), BF), pltpu.VMEM((2, tn), F32)],
        compiler_params=_cp("arbitrary", "arbitrary"),
        name="conv_in",
    )(h, w_in, w_in, w_in, conv_w, prev)
    return y, tails[tpb - 1::tpb]


def _conv_in_step_kernel(a_ref, wb_ref, wc_ref, wu_ref, cw_ref, p0_ref, p1_ref, y_ref, z_ref):
    a = a_ref[...]
    bg = _dot(a, wb_ref[...].astype(BF))
    z = _dot(a, wc_ref[...].astype(BF)) * _dot(a, wu_ref[...].astype(BF))
    cw = cw_ref[...]
    y = p0_ref[...] * cw[0:1] + p1_ref[...] * cw[1:2] + z * cw[2:3]
    y_ref[...] = (bg * y).astype(BF)
    z_ref[...] = z


def _conv_in_step(h, w_in, conv_w, p0, p1, geo):
    tm, tn = geo.m, 512
    nj = D // tn
    wspec = lambda off: pl.BlockSpec((D, tn), lambda j: (0, off * nj + j))
    blk = pl.BlockSpec((tm, tn), lambda j: (0, j))
    return pl.pallas_call(
        _conv_in_step_kernel,
        grid=(nj,),
        in_specs=[pl.BlockSpec((tm, D), lambda j: (0, 0)), wspec(0), wspec(1), wspec(2),
                  pl.BlockSpec((3, tn), lambda j: (0, j)), blk, blk],
        out_specs=[blk, blk],
        out_shape=[jax.ShapeDtypeStruct((tm, D), BF), jax.ShapeDtypeStruct((tm, D), F32)],
        compiler_params=_cp("arbitrary"),
        name="conv_in_step",
    )(h, w_in, w_in, w_in, conv_w, p0, p1)


def _mlp_kernel(a_ref, wg_ref, wu_ref, wd_ref, *rest, gated):
    o_ref = rest[-1]
    f = pl.program_id(1)
    a = a_ref[...]
    g = _dot(a, wg_ref[...])
    u = _dot(a, wu_ref[...])
    hid = (g * (1.0 / (1.0 + jnp.exp(-g)))) * u
    if gated:
        gates = rest[0][...]
        lane = lax.broadcasted_iota(jnp.int32, gates.shape, 1)
        hid = hid * jnp.sum(jnp.where(lane == f, gates, 0.0), axis=-1, keepdims=True)
    part = _dot(hid.astype(BF), wd_ref[...])

    @pl.when(f == 0)
    def _():
        o_ref[...] = part

    @pl.when(f > 0)
    def _():
        o_ref[...] += part


def _mlp(h, wg, wu, wd, geo):
    tm, tf = geo.tile(1024), 512
    return pl.pallas_call(
        functools.partial(_mlp_kernel, gated=False),
        grid=(geo.m // tm, D_FF // tf),
        in_specs=[pl.BlockSpec((tm, D), lambda i, f: (i, 0)),
                  pl.BlockSpec((D, tf), lambda i, f: (0, f)),
                  pl.BlockSpec((D, tf), lambda i, f: (0, f)),
                  pl.BlockSpec((tf, D), lambda i, f: (f, 0))],
        out_specs=pl.BlockSpec((tm, D), lambda i, f: (i, 0)),
        out_shape=jax.ShapeDtypeStruct((geo.m, D), F32),
        compiler_params=_cp("arbitrary", "arbitrary"),
        name="mlp",
    )(h, wg, wu, wd)


def _moe(h, gates, wg, wu, wd, geo):
    tm = geo.tile(256)
    return pl.pallas_call(
        functools.partial(_mlp_kernel, gated=True),
        grid=(geo.m // tm, N_EXPERTS),
        in_specs=[pl.BlockSpec((tm, D), lambda i, e: (i, 0)),
                  pl.BlockSpec((None, D, EXPERT_FF), lambda i, e: (e, 0, 0)),
                  pl.BlockSpec((None, D, EXPERT_FF), lambda i, e: (e, 0, 0)),
                  pl.BlockSpec((None, EXPERT_FF, D), lambda i, e: (e, 0, 0)),
                  pl.BlockSpec((tm, LANES), lambda i, e: (i, 0))],
        out_specs=pl.BlockSpec((tm, D), lambda i, e: (i, 0)),
        out_shape=jax.ShapeDtypeStruct((geo.m, D), F32),
        compiler_params=_cp("arbitrary", "arbitrary"),
        name="moe",
    )(h, wg, wu, wd, gates)


MOE_TM = 256
RANK_TB = 512
DISPATCH_TB = 256
COMBINE_TM = 128


def _rank_kernel(ch_ref, rank_ref, cnt_ref, carry_ref):
    @pl.when(pl.program_id(0) == 0)
    def _():
        carry_ref[...] = jnp.zeros(carry_ref.shape, F32)

    ch = ch_ref[...]
    tb = ch.shape[0]
    below = lax.broadcasted_iota(jnp.int32, (tb, tb), 1) < lax.broadcasted_iota(jnp.int32, (tb, tb), 0)
    within = _dot(jnp.where(below, 1.0, 0.0).astype(BF), ch.astype(BF))
    rank_ref[...] = within + carry_ref[...]
    carry_ref[...] += jnp.sum(ch, axis=0, keepdims=True)
    cnt_ref[...] = carry_ref[...]


def _rank(chosen):
    m = chosen.shape[0]
    blk = pl.BlockSpec((RANK_TB, LANES), lambda i: (i, 0))
    return pl.pallas_call(
        _rank_kernel,
        grid=(m // RANK_TB,),
        in_specs=[blk],
        out_specs=[blk, pl.BlockSpec((1, LANES), lambda i: (0, 0))],
        out_shape=[jax.ShapeDtypeStruct((m, LANES), F32), jax.ShapeDtypeStruct((1, LANES), F32)],
        scratch_shapes=[pltpu.VMEM((1, LANES), F32)],
        compiler_params=_cp("arbitrary"),
        name="rank",
    )(chosen)


def _route_plan(gates, chosen, rank, cnt):
    m = gates.shape[0]
    nt = 2 * m // MOE_TM
    nv = nt + N_EXPERTS - 1
    i32 = jnp.int32
    e_ids = jnp.arange(N_EXPERTS, dtype=i32)
    ends = jnp.cumsum(cnt[0, :N_EXPERTS].astype(i32))
    off = jnp.concatenate([jnp.zeros((1,), i32), ends])
    pos = off[None, :N_EXPERTS] + rank[:, :N_EXPERTS].astype(i32)
    ch = chosen[:, :N_EXPERTS] > 0
    e_a = jnp.min(jnp.where(ch, e_ids, N_EXPERTS), axis=1, keepdims=True)
    e_b = jnp.max(jnp.where(ch, e_ids, -1), axis=1, keepdims=True)
    pick = lambda a, e: jnp.sum(jnp.where(e_ids == e, a, 0), axis=1)
    g8 = gates[:, :N_EXPERTS]
    gab = jnp.pad(jnp.stack([pick(g8, e_a), pick(g8, e_b)], axis=1), ((0, 0), (0, LANES - 2)))
    lo = jnp.arange(nt, dtype=i32) * MOE_TM
    e_lo = jnp.sum(ends[None, :] <= lo[:, None], axis=1).astype(i32)
    e_hi = jnp.sum(ends[None, :] <= (lo + MOE_TM - 1)[:, None], axis=1).astype(i32)
    v_end = jnp.cumsum(e_hi - e_lo + 1)
    v_start = v_end - (e_hi - e_lo + 1)
    v = jnp.arange(nv, dtype=i32)
    tile = jnp.minimum(jnp.sum(v_end[None, :] <= v[:, None], axis=1), nt - 1).astype(i32)
    expert = jnp.minimum(e_lo[tile] + v - v_start[tile], N_EXPERTS - 1).astype(i32)
    valid = v < v_end[-1]
    first = jnp.concatenate([jnp.ones((1,), bool), tile[1:] != tile[:-1]]) & valid
    return dict(pos_a=pick(pos, e_a).astype(i32), pos_b=pick(pos, e_b).astype(i32), gab=gab, off=off,
                tile=tile, expert=expert, valid=valid.astype(i32), first=first.astype(i32))


def _dispatch_kernel(pa_ref, pb_ref, h_ref, xs_ref, sems):
    base = pl.program_id(0) * DISPATCH_TB

    def start(r, c):
        src = h_ref.at[pl.ds(r, 1), :]
        pltpu.make_async_copy(src, xs_ref.at[pl.ds(pa_ref[base + r], 1), :], sems.at[0]).start()
        pltpu.make_async_copy(src, xs_ref.at[pl.ds(pb_ref[base + r], 1), :], sems.at[0]).start()
        return c

    lax.fori_loop(0, DISPATCH_TB, start, 0)
    for _ in range(2):
        pltpu.make_async_copy(h_ref, xs_ref.at[pl.ds(0, DISPATCH_TB), :], sems.at[0]).wait()


def _dispatch(plan, h):
    m = h.shape[0]
    return pl.pallas_call(
        _dispatch_kernel,
        grid_spec=pltpu.PrefetchScalarGridSpec(
            num_scalar_prefetch=2, grid=(m // DISPATCH_TB,),
            in_specs=[pl.BlockSpec((DISPATCH_TB, D), lambda i, pa, pb: (i, 0))],
            out_specs=pl.BlockSpec(memory_space=pl.ANY),
            scratch_shapes=[pltpu.SemaphoreType.DMA((1,))]),
        out_shape=jax.ShapeDtypeStruct((2 * m, D), F32),
        compiler_params=_cp("arbitrary"),
        name="dispatch",
    )(plan["pos_a"], plan["pos_b"], h)


def _expert_mm_kernel(tile_ref, exp_ref, first_ref, valid_ref, off_ref, x_ref, wg_ref, wu_ref, wd_ref, o_ref):
    v = pl.program_id(0)

    @pl.when(valid_ref[v] == 1)
    def _():
        e = exp_ref[v]
        rows = tile_ref[v] * MOE_TM + lax.broadcasted_iota(jnp.int32, (MOE_TM, 1), 0)
        mine = (rows >= off_ref[e]) & (rows < off_ref[e + 1])
        x = x_ref[...].astype(BF)
        g = _dot(x, wg_ref[...])
        u = _dot(x, wu_ref[...])
        hid = jnp.where(mine, (g * (1.0 / (1.0 + jnp.exp(-g)))) * u, 0.0)
        part = _dot(hid.astype(BF), wd_ref[...])

        @pl.when(first_ref[v] == 1)
        def _():
            o_ref[...] = part

        @pl.when(first_ref[v] == 0)
        def _():
            o_ref[...] += part


def _expert_mm(plan, xs, wg, wu, wd):
    nv = plan["tile"].shape[0]
    row = pl.BlockSpec((MOE_TM, D), lambda v, t, e, f, va, o: (t[v], 0))
    w_in = pl.BlockSpec((None, D, EXPERT_FF), lambda v, t, e, f, va, o: (e[v], 0, 0))
    w_out = pl.BlockSpec((None, EXPERT_FF, D), lambda v, t, e, f, va, o: (e[v], 0, 0))
    return pl.pallas_call(
        _expert_mm_kernel,
        grid_spec=pltpu.PrefetchScalarGridSpec(
            num_scalar_prefetch=5, grid=(nv,), in_specs=[row, w_in, w_in, w_out], out_specs=row),
        out_shape=jax.ShapeDtypeStruct(xs.shape, F32),
        compiler_params=_cp("arbitrary"),
        name="expert_mm",
    )(plan["tile"], plan["expert"], plan["first"], plan["valid"], plan["off"], xs, wg, wu, wd)


def _combine_kernel(pa_ref, pb_ref, x_ref, gab_ref, gpost_ref, gate_ref, z_ref, o_ref, za_ref, zb_ref, sems):
    i, n = pl.program_id(0), pl.num_programs(0)
    tm = COMBINE_TM

    def start_all(step, slot):
        def body(r, c):
            t = step * tm + r
            pltpu.make_async_copy(z_ref.at[pl.ds(pa_ref[t], 1), :], za_ref.at[slot, pl.ds(r, 1), :],
                                  sems.at[slot]).start()
            pltpu.make_async_copy(z_ref.at[pl.ds(pb_ref[t], 1), :], zb_ref.at[slot, pl.ds(r, 1), :],
                                  sems.at[slot]).start()
            return c
        lax.fori_loop(0, tm, body, 0)

    @pl.when(i == 0)
    def _():
        start_all(0, 0)

    @pl.when(i + 1 < n)
    def _():
        start_all(i + 1, (i + 1) % 2)

    slot = i % 2
    for buf in (za_ref, zb_ref):
        pltpu.make_async_copy(z_ref.at[pl.ds(0, tm), :], buf.at[slot], sems.at[slot]).wait()
    gab = gab_ref[...]
    f = gab[:, 0:1] * za_ref[slot] + gab[:, 1:2] * zb_ref[slot]
    rf = lax.rsqrt(jnp.mean(f * f, axis=-1, keepdims=True) + EPS)
    o_ref[...] = x_ref[...] + gate_ref[0] * (f * rf * gpost_ref[...])


def _combine_epi(plan, x, z, g_post, gate, geo):
    mod_gate, gate_k = gate
    tm = COMBINE_TM
    tpb = geo.tb // tm
    row = pl.BlockSpec((tm, D), lambda i, pa, pb: (i, 0))
    return pl.pallas_call(
        _combine_kernel,
        grid_spec=pltpu.PrefetchScalarGridSpec(
            num_scalar_prefetch=2, grid=(geo.m // tm,),
            in_specs=[row, pl.BlockSpec((tm, LANES), lambda i, pa, pb: (i, 0)),
                      pl.BlockSpec((1, D), lambda i, pa, pb: (0, 0)),
                      pl.BlockSpec((1, geo.r, D), lambda i, pa, pb: (i // tpb, 0, gate_k)),
                      pl.BlockSpec(memory_space=pl.ANY)],
            out_specs=row,
            scratch_shapes=[pltpu.VMEM((2, tm, D), F32), pltpu.VMEM((2, tm, D), F32),
                            pltpu.SemaphoreType.DMA((2,))]),
        out_shape=jax.ShapeDtypeStruct((geo.m, D), F32),
        compiler_params=_cp("arbitrary"),
        name="combine",
    )(plan["pos_a"], plan["pos_b"], x, plan["gab"], g_post.reshape(1, D), mod_gate, z)


def _cumsum_lanes(x):
    n = x.shape[-1]
    lane = lax.broadcasted_iota(jnp.int32, x.shape, x.ndim - 1)
    s = 1
    while s < n:
        x = x + jnp.where(lane >= s, pltpu.roll(x, s, x.ndim - 1), 0.0)
        s *= 2
    return x


def _cumsum_kernel(x_ref, o_ref, *, scale):
    o_ref[0] = _cumsum_lanes(x_ref[0]) * scale


def _cumsum(lf_t, scale):
    b, h, t = lf_t.shape
    spec = pl.BlockSpec((1, h, t), lambda i: (i, 0, 0))
    return pl.pallas_call(
        functools.partial(_cumsum_kernel, scale=scale), grid=(b,), in_specs=[spec], out_specs=spec,
        out_shape=jax.ShapeDtypeStruct(lf_t.shape, F32),
        compiler_params=_cp("arbitrary"), name="cumsum",
    )(lf_t)


def _flash_kernel(q_ref, k_ref, v_ref, fk_ref, o_ref, kb_ref, vb_ref, *, tq, tk):
    qi = pl.program_id(2)

    @pl.when(qi == 0)
    def _():
        kb_ref[...] = k_ref[...].astype(BF)
        vb_ref[...] = v_ref[...].astype(BF)

    q = q_ref[...]

    def step(kj, carry, masked):
        m, l, acc = carry
        start = pl.multiple_of(kj * tk, tk)
        k = kb_ref[pl.ds(start, tk), :]
        v = vb_ref[pl.ds(start, tk), :]
        s = lax.dot_general(q, k, NT_DIMS, preferred_element_type=F32) - fk_ref[0, kj]
        if masked:
            row = lax.broadcasted_iota(jnp.int32, s.shape, 0) + qi * tq
            col = lax.broadcasted_iota(jnp.int32, s.shape, 1) + kj * tk
            s = jnp.where(col <= row, s, -jnp.inf)
        m_new = jnp.maximum(m, jnp.max(s, axis=-1, keepdims=True))
        alpha = jnp.exp2(m - m_new)
        p = jnp.exp2(s - m_new)
        l = alpha * l + jnp.sum(p, axis=-1, keepdims=True)
        acc = alpha * acc + _dot(p.astype(BF), v)
        return m_new, l, acc

    init = (jnp.full((tq, 1), -jnp.inf, F32), jnp.zeros((tq, 1), F32), jnp.zeros((tq, HEAD_DIM), F32))
    n_full = (qi * tq) // tk
    carry = lax.fori_loop(0, n_full, lambda kj, c: step(kj, c, False), init)
    _, l, acc = step(n_full, carry, True)
    o_ref[...] = (acc / l).astype(o_ref.dtype)


FLASH_TQ = 512
FLASH_TK = 512


def _flash(q, k, v, fk, nb, t):
    tq, tk = FLASH_TQ, FLASH_TK
    assert tk % tq == 0
    nq = t // tq
    return pl.pallas_call(
        functools.partial(_flash_kernel, tq=tq, tk=tk),
        grid=(nb, N_HEADS, nq),
        in_specs=[pl.BlockSpec((tq, HEAD_DIM), lambda b, h, i: (b * nq + i, h)),
                  pl.BlockSpec((t, HEAD_DIM), lambda b, h, i: (b, h)),
                  pl.BlockSpec((t, HEAD_DIM), lambda b, h, i: (b, h)),
                  pl.BlockSpec((1, t // tk, 1, tk), lambda b, h, i: (b * N_HEADS + h, 0, 0, 0))],
        out_specs=pl.BlockSpec((tq, HEAD_DIM), lambda b, h, i: (b * nq + i, h)),
        out_shape=jax.ShapeDtypeStruct((nb * t, D), BF),
        scratch_shapes=[pltpu.VMEM((t, HEAD_DIM), BF), pltpu.VMEM((t, HEAD_DIM), BF)],
        compiler_params=_cp("arbitrary", "arbitrary", "arbitrary"),
        name="flash",
    )(q, k, v, fk)


PAGE_COLS = PAGE * N_HEADS
PAGES_PER_STEP = 4


def _head_cumsum(x):
    lane = lax.broadcasted_iota(jnp.int32, x.shape, 1)
    s = N_HEADS
    while s < PAGE_COLS:
        x = x + jnp.where(lane >= s, pltpu.roll(x, s, 1), 0.0)
        s *= 2
    return x


def _head_totals(x):
    lane = lax.broadcasted_iota(jnp.int32, x.shape, 1)
    y = jnp.where(lane >= PAGE_COLS - N_HEADS, x, 0.0)
    s = N_HEADS
    while s < PAGE_COLS:
        y = y + pltpu.roll(y, PAGE_COLS - s, 1)
        s *= 2
    return y


def _decode_kernel(pt_ref, q_ref, kn_ref, vn_ref, lfn_ref, *refs):
    n = PAGES_PER_STEP
    k_refs, v_refs, lf_refs = refs[:n], refs[n:2 * n], refs[2 * n:3 * n]
    o_ref, qb_ref, m_ref, l_ref, acc_ref, fc_ref, kb_ref, vb_ref = refs[3 * n:]
    p = pl.program_id(1)
    lane = lax.broadcasted_iota(jnp.int32, (N_HEADS, n * PAGE_COLS), 1)
    head = lax.broadcasted_iota(jnp.int32, (N_HEADS, n * PAGE_COLS), 0)
    own = lane % N_HEADS == head

    @pl.when(p == 0)
    def _():
        qb_ref[...] = q_ref[0].astype(BF)
        m_ref[...] = jnp.full(m_ref.shape, -jnp.inf, F32)
        l_ref[...] = jnp.zeros(l_ref.shape, F32)
        acc_ref[...] = jnp.zeros(acc_ref.shape, F32)
        fc_ref[...] = jnp.zeros(fc_ref.shape, F32)

    qb = qb_ref[...]
    for i in range(n):
        kb_ref[i * PAGE_COLS:(i + 1) * PAGE_COLS, :] = k_refs[i][0].astype(BF)
        vb_ref[i * PAGE_COLS:(i + 1) * PAGE_COLS, :] = v_refs[i][0].astype(BF)
    s = lax.dot_general(qb, kb_ref[...], NT_DIMS, preferred_element_type=F32)
    cs = _head_cumsum(jnp.concatenate([r[0] for r in lf_refs], axis=0))
    tot = _head_totals(cs)
    run = fc_ref[...]
    fps = []
    for i in range(n):
        fps.append(cs[i:i + 1] + run)
        run = run + tot[i:i + 1]
    fc = run
    s = jnp.where(own, s * ATTN_SCALE - jnp.concatenate(fps, axis=1), -jnp.inf)
    m_old = m_ref[...]
    m = jnp.maximum(m_old, jnp.max(s, axis=-1, keepdims=True))
    alpha = jnp.exp(m_old - m)
    pr = jnp.exp(s - m[:, 0:1])
    l = alpha * l_ref[...] + jnp.sum(pr, axis=-1, keepdims=True)
    acc = alpha * acc_ref[...] + _dot(pr.astype(BF), vb_ref[...])
    m_ref[...] = m
    l_ref[...] = l
    acc_ref[...] = acc
    fc_ref[...] = fc

    @pl.when(p == pl.num_programs(1) - 1)
    def _():
        kn = kn_ref[0].astype(BF).astype(F32)
        vn = vn_ref[0].astype(BF).astype(F32)
        on_head = (lax.broadcasted_iota(jnp.int32, (N_HEADS, PAGE_COLS), 1)
                   == lax.broadcasted_iota(jnp.int32, (N_HEADS, PAGE_COLS), 0))
        f_past = jnp.sum(jnp.where(on_head, fc, 0.0), axis=-1, keepdims=True)
        s_n = jnp.sum(qb.astype(F32) * kn, axis=-1, keepdims=True) * ATTN_SCALE
        s_n = s_n - (f_past + lfn_ref[0])
        m1 = m[:, 0:1]
        m2 = jnp.maximum(m1, s_n)
        a2 = jnp.exp(m1 - m2)
        pn = jnp.exp(s_n - m2)
        l2 = a2 * l[:, 0:1] + pn
        acc2 = a2 * acc + pn.astype(BF).astype(F32) * vn
        o_ref[0] = (acc2 / l2).astype(o_ref.dtype)


def _decode(page_table, q, kn, vn, lfn, cache_k, cache_v, cache_lf):
    nb, n_pages = page_table.shape
    n = PAGES_PER_STEP
    tok = pl.BlockSpec((1, N_HEADS, HEAD_DIM), lambda b, p, pt: (b, 0, 0))
    page = lambda i: pl.BlockSpec((1, PAGE_COLS, HEAD_DIM), lambda b, p, pt: (pt[b, n * p + i], 0, 0))
    lfpage = lambda i: pl.BlockSpec((1, 1, PAGE_COLS), lambda b, p, pt: (pt[b, n * p + i], 0, 0))
    grid_spec = pltpu.PrefetchScalarGridSpec(
        num_scalar_prefetch=1,
        grid=(nb, n_pages // n),
        in_specs=[tok, tok, tok, pl.BlockSpec((1, N_HEADS, 1), lambda b, p, pt: (b, 0, 0))]
        + [page(i) for i in range(n)] + [page(i) for i in range(n)] + [lfpage(i) for i in range(n)],
        out_specs=tok,
        scratch_shapes=[pltpu.VMEM((N_HEADS, HEAD_DIM), BF),
                        pltpu.VMEM((N_HEADS, LANES), F32),
                        pltpu.VMEM((N_HEADS, LANES), F32),
                        pltpu.VMEM((N_HEADS, HEAD_DIM), F32),
                        pltpu.VMEM((1, PAGE_COLS), F32),
                        pltpu.VMEM((n * PAGE_COLS, HEAD_DIM), BF),
                        pltpu.VMEM((n * PAGE_COLS, HEAD_DIM), BF)],
    )
    return pl.pallas_call(
        _decode_kernel,
        grid_spec=grid_spec,
        out_shape=jax.ShapeDtypeStruct((nb, N_HEADS, HEAD_DIM), BF),
        compiler_params=_cp("arbitrary", "arbitrary"),
        name="decode",
    )(page_table, q, kn, vn, lfn, *([cache_k] * n), *([cache_v] * n), *([cache_lf] * n))


def _moe_dense(x, h, gates, chosen, g_post, gate, w, geo):
    del chosen
    f = _moe(h, gates, w["wg_e"], w["wu_e"], w["wd_e"], geo)
    (x,) = _epi(x, f, g_post, gate, [], geo)
    return x


def _moe_top2(x, h, gates, chosen, g_post, gate, w, geo):
    rank, cnt = _rank(chosen)
    plan = _route_plan(gates, chosen, rank, cnt)
    z = _expert_mm(plan, _dispatch(plan, h), w["wg_e"], w["wu_e"], w["wd_e"])
    return _combine_epi(plan, x, z, g_post, gate, geo)


_moe_dense.h_dtype = BF
_moe_top2.h_dtype = F32


def _trunk(x, geo, mod0, mod1, modkv, w, conv_fn, attend_fn, moe_fn):
    g_norm = w["g_norm"]
    h = _norm_mod(x, g_norm[0, 0], mod0, 1, 0, geo)
    y, conv_state = conv_fn(h)
    x, h = _epi(x, (y, w["w_out"]), g_norm[0, 1], (mod0, 2), [(g_norm[0, 2], mod0, 4, 3)], geo)
    f = _mlp(h, w["wg_d"], w["wu_d"], w["wd_d"], geo)
    x, hq, hkv = _epi(x, f, g_norm[0, 3], (mod0, 5),
                      [(g_norm[1, 0], mod1, 1, 0), (w["g_kv"], modkv, 1, 0)], geo)
    k = _mm(hkv, w["w_kv"], F32, geo)
    v = _mm(hkv, w["w_kv"], F32, geo, col0=D)
    lf = _logf(hkv, w["wf"], w["bf"], geo)[:, :N_HEADS]
    q = _mm(hq, w["wq"], attend_fn.q_dtype, geo, scale=attend_fn.q_scale)
    att = attend_fn(q, k, v, lf)
    x, h, gates, chosen = _epi(x, (att, w["wo"]), g_norm[1, 1], (mod1, 2), [(g_norm[1, 2], mod1, 4, 3)], geo,
                               router=(w["wr"], w["br"]), h_dtype=moe_fn.h_dtype)
    x = moe_fn(x, h, gates, chosen, g_norm[1, 3], (mod1, 5), w, geo)
    return x, conv_state, k, v, lf


def kernel(x_prompt, x_sample, state_conv, cache_k, cache_v, cache_logf, page_table, c_prompt, c_sample,
           w_mod, b_mod, g_norm, w_in_a, conv_w, w_out_a, g_kv, w_mod_kv, b_mod_kv, w_kv, b_f,
           w_q, w_o, w_gate_d, w_up_d, w_down_d, w_router, b_router, w_gate_e, w_up_e, w_down_e):
    nb_p, t_p, _ = x_prompt.shape
    nb_s = x_sample.shape[0]
    pad_s = SAMPLE_ROWS - nb_s
    n_phys = cache_k.shape[0]

    w = {
        "g_norm": g_norm, "g_kv": g_kv,
        "w_out": w_out_a[0].astype(BF),
        "w_kv": w_kv,
        "wf": jnp.pad(w_kv[:, 2 * D:], ((0, 0), (0, LANES - N_HEADS))).astype(BF),
        "bf": jnp.pad(b_f, (0, LANES - N_HEADS)).reshape(1, LANES),
        "wq": w_q.reshape(D, D), "wo": w_o[0].astype(BF),
        "wg_d": w_gate_d[0].astype(BF), "wu_d": w_up_d[0].astype(BF), "wd_d": w_down_d[0].astype(BF),
        "wr": jnp.pad(w_router[0], ((0, 0), (0, LANES - N_EXPERTS))),
        "br": jnp.pad(b_router[0], (0, LANES - N_EXPERTS)).reshape(1, LANES),
        "wg_e": w_gate_e[0].astype(BF), "wu_e": w_up_e[0].astype(BF), "wd_e": w_down_e[0].astype(BF),
    }
    w_in = w_in_a.reshape(D, 3 * D)
    cw = conv_w[0]

    c16 = jnp.concatenate([c_prompt, c_sample, jnp.zeros((SAMPLE_ROWS - nb_p - nb_s, D), F32)], axis=0)
    mod = _modvec(c16, w_mod, b_mod)
    modkv = _modvec(c16, w_mod_kv[None], b_mod_kv[None])[0]

    def split(m):
        m_p = m[:nb_p][:, None, :]
        m_s = jnp.pad(m[nb_p:nb_p + nb_s], ((0, pad_s), (0, 0)))[None]
        return m_p, m_s

    mod0_p, mod0_s = split(mod[0])
    mod1_p, mod1_s = split(mod[1])
    modkv_p, modkv_s = split(modkv)

    geo_p = Geo(nb_p * t_p, t_p, 1)
    zeros_prev = jnp.zeros((nb_p, 2, D), F32)

    def attend_p(q, k, v, lf):
        lf_t = jnp.transpose(lf.reshape(nb_p, t_p, N_HEADS), (0, 2, 1))
        fk = _cumsum(lf_t, LOG2E).reshape(nb_p * N_HEADS, t_p // FLASH_TK, 1, FLASH_TK)
        return _flash(q, k, v, fk, nb_p, t_p)

    attend_p.q_dtype, attend_p.q_scale = BF, ATTN_SCALE * LOG2E
    y_p, conv_p, k_p, v_p, lf_p = _trunk(
        x_prompt.reshape(nb_p * t_p, D), geo_p, mod0_p, mod1_p, modkv_p, w,
        lambda h: _conv_in(h, w_in, cw, zeros_prev, geo_p), attend_p, _moe_top2)

    geo_s = Geo(SAMPLE_ROWS, SAMPLE_ROWS, SAMPLE_ROWS)
    pad_rows = lambda a: jnp.pad(a, ((0, pad_s), (0, 0)))
    prev0 = pad_rows(state_conv[0, :, 0])
    prev1 = pad_rows(state_conv[0, :, 1])
    ck = cache_k.reshape(n_phys, PAGE_COLS, HEAD_DIM)
    cv = cache_v.reshape(n_phys, PAGE_COLS, HEAD_DIM)
    clf = cache_logf.reshape(n_phys, 1, PAGE_COLS)
    heads = lambda a: a[:nb_s].reshape(nb_s, N_HEADS, HEAD_DIM)

    def attend_s(q, k, v, lf):
        o = _decode(page_table, heads(q), heads(k), heads(v), lf[:nb_s, :, None], ck, cv, clf)
        return pad_rows(o.reshape(nb_s, D))

    attend_s.q_dtype, attend_s.q_scale = F32, 1.0
    y_s, z_s, k_s, v_s, lf_s = _trunk(
        pad_rows(x_sample[:, 0, :]), geo_s, mod0_s, mod1_s, modkv_s, w,
        lambda h: _conv_in_step(h, w_in, cw, prev0, prev1, geo_s), attend_s, _moe_dense)

    conv_s = jnp.stack([state_conv[0, :, 1], z_s[:nb_s]], axis=1)[None]
    return (y_p.reshape(nb_p, t_p, D),
            y_s[:nb_s].reshape(nb_s, 1, D),
            conv_p[None],
            conv_s,
            k_p.reshape(nb_p, t_p, N_HEADS, HEAD_DIM),
            v_p.reshape(nb_p, t_p, N_HEADS, HEAD_DIM),
            lf_p.reshape(nb_p, t_p, N_HEADS),
            k_s[:nb_s].reshape(nb_s, 1, N_HEADS, HEAD_DIM),
            v_s[:nb_s].reshape(nb_s, 1, N_HEADS, HEAD_DIM),
            lf_s[:nb_s].reshape(nb_s, 1, N_HEADS))
```
